```python
import math
import jax, jax.numpy as jnp
from jax import lax
import numpy as np

D_MODEL = 2048
BATCH = 4
SEQ = 4096
DEPTH = 4

HEAD_DIM = 128
RET_HEADS = D_MODEL // 512
MLSTM_HEADS = D_MODEL // 512
FOX_HEADS = D_MODEL // 256
D_RET = RET_HEADS * HEAD_DIM
D_MLSTM = MLSTM_HEADS * HEAD_DIM
D_FOX = FOX_HEADS * HEAD_DIM
D_MIX = D_RET + D_MLSTM + D_FOX
CHUNK = 128
Q_BLOCK = 128
CONV_WIDTH = 4
ROPE_BASE = 10000.0
N_GROUPS = 4
EXPERTS_PER_GROUP = 8
N_EXPERTS = N_GROUPS * EXPERTS_PER_GROUP
TOP_K_IN_GROUP = 2
D_EXPERT = D_MODEL // 4
N_MOD = 6
EPS = 1e-6
SPLIT_SIZES = (D_RET, D_RET, D_RET, D_RET,
               D_MLSTM, D_MLSTM, D_MLSTM, MLSTM_HEADS, MLSTM_HEADS,
               D_FOX, D_FOX, D_FOX, FOX_HEADS)
N_IN = 4 * D_RET + 3 * D_MLSTM + 2 * MLSTM_HEADS + 3 * D_FOX + FOX_HEADS

kernel_name = "hybrid_ret_mlstm_fox_hmoe_adaln"


def rmsnorm(x, g):
    xf = x.astype(jnp.float32)
    y = xf * lax.rsqrt(jnp.mean(xf * xf, axis=-1, keepdims=True) + EPS)
    return (y * g.astype(jnp.float32)).astype(x.dtype)


def head_groupnorm(y, g, out_dtype):
    B, S, H, d = y.shape
    yf = y.astype(jnp.float32)
    mu = jnp.mean(yf, axis=-1, keepdims=True)
    var = jnp.mean(jnp.square(yf - mu), axis=-1, keepdims=True)
    yn = (yf - mu) * lax.rsqrt(var + EPS)
    return (yn * g.astype(jnp.float32).reshape(H, d)).astype(out_dtype).reshape(B, S, H * d)


def rotary(x, pos):
    half = x.shape[-1] // 2
    inv = ROPE_BASE ** (-jnp.arange(half, dtype=jnp.float32) / half)
    ang = pos[:, None] * inv[None, :]
    cos = jnp.cos(ang)[None, :, None, :]
    sin = jnp.sin(ang)[None, :, None, :]
    x1, x2 = x[..., :half], x[..., half:]
    return jnp.concatenate([x1 * cos - x2 * sin, x1 * sin + x2 * cos], axis=-1).astype(x.dtype)


def to_chunks(t):
    B, S, H, d = t.shape
    return t.reshape(B, S // CHUNK, CHUNK, H, d).transpose(0, 1, 3, 2, 4)


def from_chunks(t):
    B, nC, H, L, d = t.shape
    return t.transpose(0, 1, 3, 2, 4).reshape(B, nC * L, H, d)


def retention_chunkwise(q, k, v):
    H, d = q.shape[2], q.shape[3]
    log_gamma = jnp.log1p(-jnp.power(2.0, -5.0 - jnp.arange(H, dtype=jnp.float32)))
    pos = jnp.arange(CHUNK, dtype=jnp.float32)
    diff = pos[:, None] - pos[None, :]
    decay = jnp.where(diff >= 0, jnp.exp(jnp.maximum(diff, 0.0)[None] * log_gamma[:, None, None]), 0.0)
    zeta = jnp.exp((CHUNK - 1 - pos)[None, :] * log_gamma[:, None])
    xi = jnp.exp((pos + 1)[None, :] * log_gamma[:, None])
    gamma_chunk = jnp.exp(CHUNK * log_gamma)
    q_c, k_c, v_c = to_chunks(q), to_chunks(k * d ** -0.5), to_chunks(v)
    s = jnp.einsum('bchid,bchjd->bchij', q_c, k_c) * decay
    intra = jnp.einsum('bchij,bchje->bchie', s, v_c)
    kv = jnp.einsum('bchjd,hj,bchje->bchde', k_c, zeta, v_c)

    def step(R, kv_i):
        return R * gamma_chunk[None, :, None, None] + kv_i, R

    _, R_prev = lax.scan(step, jnp.zeros_like(kv[:, 0]), kv.swapaxes(0, 1))
    R_prev = R_prev.swapaxes(0, 1)
    cross = jnp.einsum('bchid,bchde->bchie', q_c, R_prev) * xi[None, None, :, :, None]
    return from_chunks(intra + cross)


def mlstm_chunkwise(q, k, v, i_pre, f_pre):
    B, S, H, d = q.shape
    nC = S // CHUNK
    q_c, k_c, v_c = to_chunks(q), to_chunks(k * d ** -0.5), to_chunks(v)

    def gate_chunks(t):
        return t.astype(jnp.float32).reshape(B, nC, CHUNK, H).transpose(0, 1, 3, 2)

    log_f = jax.nn.log_sigmoid(gate_chunks(f_pre))
    log_i = gate_chunks(i_pre)
    b = jnp.cumsum(log_f, axis=-1)
    b_end = b[..., -1]
    causal = jnp.tril(jnp.ones((CHUNK, CHUNK), dtype=bool))
    d_log = jnp.where(causal, b[..., :, None] - b[..., None, :] + log_i[..., None, :], -jnp.inf)
    a = b_end[..., None] - b + log_i
    a_max = jnp.max(a, axis=-1)
    w = jnp.exp(a - a_max[..., None])
    kv = jnp.einsum('bchj,bchjd,bchje->bchde', w, k_c, v_c)
    ks = jnp.einsum('bchj,bchjd->bchd', w, k_c)

    def step(carry, inp):
        C, n, m = carry
        kv_i, ks_i, amax_i, bend_i = inp
        m_new = jnp.maximum(bend_i + m, amax_i)
        s_old = jnp.exp(bend_i + m - m_new)
        s_new = jnp.exp(amax_i - m_new)
        C_new = s_old[..., None, None] * C + s_new[..., None, None] * kv_i
        n_new = s_old[..., None] * n + s_new[..., None] * ks_i
        return (C_new, n_new, m_new), (C, n, m)

    init = (jnp.zeros((B, H, d, d), kv.dtype), jnp.zeros((B, H, d), ks.dtype),
            jnp.full((B, H), -jnp.inf, jnp.float32))
    xs = (kv.swapaxes(0, 1), ks.swapaxes(0, 1), a_max.swapaxes(0, 1), b_end.swapaxes(0, 1))
    _, (C_prev, n_prev, m_prev) = lax.scan(step, init, xs)
    C_prev, n_prev, m_prev = C_prev.swapaxes(0, 1), n_prev.swapaxes(0, 1), m_prev.swapaxes(0, 1)
    inter_log = b + m_prev[..., None]
    m_t = jnp.maximum(jnp.max(d_log, axis=-1), inter_log)
    d_w = jnp.exp(d_log - m_t[..., None])
    inter_w = jnp.exp(inter_log - m_t)
    s = jnp.einsum('bchid,bchjd->bchij', q_c, k_c) * d_w
    num = (jnp.einsum('bchij,bchje->bchie', s, v_c)
           + inter_w[..., None] * jnp.einsum('bchid,bchde->bchie', q_c, C_prev))
    den = jnp.sum(s, axis=-1) + inter_w * jnp.einsum('bchid,bchd->bchi', q_c, n_prev)
    h = num / jnp.maximum(jnp.abs(den), jnp.exp(-m_t))[..., None]
    return from_chunks(h)


def forgetting_attention(q, k, v, f_pre):
    B, S, H, d = q.shape
    nQ = S // Q_BLOCK
    F = jnp.cumsum(jax.nn.log_sigmoid(f_pre.astype(jnp.float32)), axis=1).transpose(0, 2, 1)
    k_t = k.transpose(0, 2, 1, 3)
    v_t = v.transpose(0, 2, 1, 3)
    q_b = q.reshape(B, nQ, Q_BLOCK, H, d).transpose(1, 0, 3, 2, 4)
    F_b = F.reshape(B, H, nQ, Q_BLOCK).transpose(2, 0, 1, 3)
    k_pos = jnp.arange(S)
    scale = d ** -0.5

    def block(args):
        q_i, F_i, blk = args
        logits = (jnp.einsum('bhqd,bhkd->bhqk', q_i, k_t).astype(jnp.float32) * scale
                  + F_i[..., :, None] - F[:, :, None, :])
        q_pos = blk * Q_BLOCK + jnp.arange(Q_BLOCK)
        logits = jnp.where(k_pos[None, :] <= q_pos[:, None], logits, -jnp.inf)
        p = jax.nn.softmax(logits, axis=-1)
        return jnp.einsum('bhqk,bhkd->bhqd', p.astype(v_t.dtype), v_t)

    out = lax.map(block, (q_b, F_b, jnp.arange(nQ)))
    return out.transpose(1, 0, 3, 2, 4).reshape(B, S, H, d)


def causal_depthwise_conv(x, w, bias):
    y = lax.conv_general_dilated(x, w[:, None, :], window_strides=(1,),
                                 padding=[(CONV_WIDTH - 1, 0)],
                                 dimension_numbers=('NWC', 'WIO', 'NWC'),
                                 feature_group_count=x.shape[-1])
    return y + bias


def mixing_layer(h, w_in, ret_gn_g, conv_w, conv_b, m_wq, m_wk, m_i_b, m_f_b, m_gn_g, fox_f_b, w_out, pos):
    B, S, _ = h.shape
    proj = h @ w_in
    parts, o = [], 0
    for n in SPLIT_SIZES:
        parts.append(proj[..., o:o + n])
        o += n
    rq, rk, rv, rg, mx, mv, mo, mi, mf, fq, fk, fv, ff = parts

    def heads(t, H):
        return t.reshape(B, S, H, HEAD_DIM)

    y_ret = retention_chunkwise(rotary(heads(rq, RET_HEADS), pos), rotary(heads(rk, RET_HEADS), pos),
                                heads(rv, RET_HEADS))
    y_ret = head_groupnorm(y_ret, ret_gn_g, h.dtype) * jax.nn.silu(rg)

    xc = heads(jax.nn.silu(causal_depthwise_conv(mx, conv_w, conv_b)), MLSTM_HEADS)
    mq = jnp.einsum('bshd,hde->bshe', xc, m_wq)
    mk = jnp.einsum('bshd,hde->bshe', xc, m_wk)
    y_m = mlstm_chunkwise(mq, mk, heads(mv, MLSTM_HEADS), mi + m_i_b, mf + m_f_b)
    y_m = jax.nn.sigmoid(mo) * head_groupnorm(y_m, m_gn_g, h.dtype)

    y_f = forgetting_attention(heads(fq, FOX_HEADS), heads(fk, FOX_HEADS), heads(fv, FOX_HEADS),
                               ff + fox_f_b).reshape(B, S, D_FOX)

    y = jnp.concatenate([y_ret.astype(h.dtype), y_m.astype(h.dtype), y_f.astype(h.dtype)], axis=-1)
    return (y @ w_out).astype(h.dtype)


def hierarchical_moe(h, rg_w, rg_b, re_w, re_b, w1, w3, w2):
    B, S, D = h.shape
    t = h.reshape(-1, D)
    T = t.shape[0]
    g_prob = jax.nn.softmax((t @ rg_w + rg_b).astype(jnp.float32), axis=-1)
    g_val, g_idx = lax.top_k(g_prob, 1)
    e_logits = (t @ re_w + re_b).astype(jnp.float32).reshape(T, N_GROUPS, EXPERTS_PER_GROUP)
    e_sel = jnp.take_along_axis(e_logits, g_idx[:, :, None], axis=1)[:, 0]
    e_prob = jax.nn.softmax(e_sel, axis=-1)
    e_val, e_idx = lax.top_k(e_prob, TOP_K_IN_GROUP)
    e_val = e_val / jnp.sum(e_val, axis=-1, keepdims=True)
    gate = g_val * e_val
    expert_id = g_idx * EXPERTS_PER_GROUP + e_idx
    combine = jnp.sum(jax.nn.one_hot(expert_id, N_EXPERTS, dtype=jnp.float32) * gate[..., None], axis=1)
    combine = combine.astype(t.dtype)
    out = jnp.zeros_like(t)
    for e in range(N_EXPERTS):
        y_e = (jax.nn.silu(t @ w1[e]) * (t @ w3[e])) @ w2[e]
        out = out + combine[:, e:e + 1] * y_e
    return out.reshape(B, S, D)


def setup_inputs(seed: int = 0) -> dict:
    key = jax.random.key(seed)
    ks = jax.random.split(key, 26)

    def nrm(k, shape, scale):
        return scale * jax.random.normal(k, shape, jnp.float32)

    L, D = DEPTH, D_MODEL
    return {
        "x": nrm(ks[0], (BATCH, SEQ, D), 1.0),
        "c": nrm(ks[1], (BATCH, D), 1.0),
        "ada_w": nrm(ks[2], (L, D, N_MOD * D), 0.5 * D ** -0.5),
        "ada_b": nrm(ks[3], (L, N_MOD * D), 0.01),
        "norm1_g": 1.0 + nrm(ks[4], (L, D), 0.05),
        "w_in": nrm(ks[5], (L, D, N_IN), D ** -0.5),
        "ret_gn_g": 1.0 + nrm(ks[6], (L, D_RET), 0.05),
        "mlstm_conv_w": nrm(ks[7], (L, CONV_WIDTH, D_MLSTM), CONV_WIDTH ** -0.5),
        "mlstm_conv_b": nrm(ks[8], (L, D_MLSTM), 0.01),
        "mlstm_wq": nrm(ks[9], (L, MLSTM_HEADS, HEAD_DIM, HEAD_DIM), HEAD_DIM ** -0.5),
        "mlstm_wk": nrm(ks[10], (L, MLSTM_HEADS, HEAD_DIM, HEAD_DIM), HEAD_DIM ** -0.5),
        "mlstm_i_b": nrm(ks[11], (L, MLSTM_HEADS), 0.1),
        "mlstm_f_b": jnp.linspace(3.0, 6.0, MLSTM_HEADS, dtype=jnp.float32)[None, :] + nrm(ks[12], (L, MLSTM_HEADS), 0.1),
        "mlstm_gn_g": 1.0 + nrm(ks[13], (L, D_MLSTM), 0.05),
        "fox_f_b": jnp.linspace(2.0, 5.0, FOX_HEADS, dtype=jnp.float32)[None, :] + nrm(ks[14], (L, FOX_HEADS), 0.1),
        "w_out": nrm(ks[15], (L, D_MIX, D), D_MIX ** -0.5),
        "norm2_g": 1.0 + nrm(ks[16], (L, D), 0.05),
        "router_group_w": nrm(ks[17], (L, D, N_GROUPS), D ** -0.5),
        "router_group_b": nrm(ks[18], (L, N_GROUPS), 0.01),
        "router_expert_w": nrm(ks[19], (L, D, N_EXPERTS), D ** -0.5),
        "router_expert_b": nrm(ks[20], (L, N_EXPERTS), 0.01),
        "moe_w1": nrm(ks[21], (L, N_EXPERTS, D, D_EXPERT), D ** -0.5),
        "moe_w3": nrm(ks[22], (L, N_EXPERTS, D, D_EXPERT), D ** -0.5),
        "moe_w2": nrm(ks[23], (L, N_EXPERTS, D_EXPERT, D), D_EXPERT ** -0.5),
        "final_g": 1.0 + nrm(ks[24], (D,), 0.05),
    }


def reference(x, c, ada_w, ada_b, norm1_g, w_in, ret_gn_g, mlstm_conv_w, mlstm_conv_b, mlstm_wq, mlstm_wk,
              mlstm_i_b, mlstm_f_b, mlstm_gn_g, fox_f_b, w_out, norm2_g, router_group_w, router_group_b,
              router_expert_w, router_expert_b, moe_w1, moe_w3, moe_w2, final_g):
    S = x.shape[1]
    pos = jnp.arange(S, dtype=jnp.float32)
    c_act = jax.nn.silu(c)
    for l in range(DEPTH):
        mod = c_act @ ada_w[l] + ada_b[l]
        sh1, sc1, g1, sh2, sc2, g2 = jnp.split(mod[:, None, :], N_MOD, axis=-1)
        h = rmsnorm(x, norm1_g[l]) * (1.0 + sc1) + sh1
        x = x + g1 * mixing_layer(h, w_in[l], ret_gn_g[l], mlstm_conv_w[l], mlstm_conv_b[l], mlstm_wq[l],
                                  mlstm_wk[l], mlstm_i_b[l], mlstm_f_b[l], mlstm_gn_g[l], fox_f_b[l],
                                  w_out[l], pos)
        h = rmsnorm(x, norm2_g[l]) * (1.0 + sc2) + sh2
        x = x + g2 * hierarchical_moe(h, router_group_w[l], router_group_b[l], router_expert_w[l],
                                      router_expert_b[l], moe_w1[l], moe_w3[l], moe_w2[l])
    return rmsnorm(x, final_g)
```

```python
import functools

import jax
import jax.numpy as jnp
from jax import lax
from jax.experimental import pallas as pl
from jax.experimental.pallas import tpu as pltpu

F32 = jnp.float32
BF16 = jnp.bfloat16

HEAD_DIM = 128
LANES = 128
RET_HEADS = 4
MLSTM_HEADS = 4
FOX_HEADS = 8
D_RET = RET_HEADS * HEAD_DIM
D_MLSTM = MLSTM_HEADS * HEAD_DIM
D_FOX = FOX_HEADS * HEAD_DIM
CHUNK = 128
CONV_WIDTH = 4
ROPE_BASE = 10000.0
N_GROUPS = 4
EXPERTS_PER_GROUP = 8
N_EXPERTS = N_GROUPS * EXPERTS_PER_GROUP
N_MOD = 6
EPS = 1e-6
NEG_INF = float("-inf")

COL_RQ, COL_RK, COL_RV, COL_RG = 0, 512, 1024, 1536
COL_MX, COL_MV, COL_MO = 2048, 2560, 3072
COL_FQ, COL_FK, COL_FV = 3584, 4608, 5632
N_PROJ = 6656
LANE_MI, LANE_MF, LANE_FF = 0, 4, 8
LANE_RE, LANE_RG = 0, 32

VMEM_LIMIT = 56 * 1024 * 1024
MOE_TILE = 512


def _cparams(n_grid):
    return pltpu.CompilerParams(dimension_semantics=("arbitrary",) * n_grid,
                                vmem_limit_bytes=VMEM_LIMIT)


def _dot(a, b):
    return jnp.dot(a, b, preferred_element_type=F32)


def _dot_nt(a, b):
    return lax.dot_general(a, b, (((1,), (1,)), ((), ())), preferred_element_type=F32)


def _silu(x):
    return x * jax.nn.sigmoid(x)


def _log_sigmoid(x):
    return jnp.minimum(x, 0.0) - jnp.log(1.0 + jnp.exp(-jnp.abs(x)))


def _split3(x):
    hi = x.astype(BF16)
    r1 = x - hi.astype(F32)
    mid = r1.astype(BF16)
    lo = (r1 - mid.astype(F32)).astype(BF16)
    return hi, mid, lo


def _cumsum_rows(tri, x):
    hi, mid, lo = _split3(x)
    return _dot(tri, hi) + _dot(tri, mid) + _dot(tri, lo)


def _lower_tri(n, dtype):
    r = lax.broadcasted_iota(jnp.int32, (n, n), 0)
    c = lax.broadcasted_iota(jnp.int32, (n, n), 1)
    return jnp.where(c <= r, 1.0, 0.0).astype(dtype)


def _group_norm(y, gain):
    mu = jnp.mean(y, axis=-1, keepdims=True)
    yc = y - mu
    var = jnp.mean(yc * yc, axis=-1, keepdims=True)
    return yc * lax.rsqrt(var + EPS) * gain


def _mod_kernel(c_ref, w_ref, b_ref, o_ref, *, nb, tn):
    w = w_ref[0]
    rows = []
    for b in range(nb):
        cb = _silu(c_ref[b])
        parts = [jnp.sum(w[:, j * LANES:(j + 1) * LANES] * cb, axis=0, keepdims=True)
                 for j in range(tn // LANES)]
        rows.append(jnp.concatenate(parts, axis=1))
    o_ref[0] = jnp.concatenate(rows, axis=0) + b_ref[0]


def _modulation(c, ada_w, ada_b):
    depth, d, n = ada_w.shape
    nb = c.shape[0]
    tn = 512
    c_b = jnp.broadcast_to(c[:, :, None], (nb, d, LANES))
    return pl.pallas_call(
        functools.partial(_mod_kernel, nb=nb, tn=tn),
        grid=(depth, n // tn),
        in_specs=[pl.BlockSpec((nb, d, LANES), lambda l, j: (0, 0, 0)),
                  pl.BlockSpec((1, d, tn), lambda l, j: (l, 0, j)),
                  pl.BlockSpec((1, 1, tn), lambda l, j: (l, 0, j))],
        out_specs=pl.BlockSpec((1, nb, tn), lambda l, j: (l, 0, j)),
        out_shape=jax.ShapeDtypeStruct((depth, nb, n), F32),
        compiler_params=_cparams(2),
        name="adaln_mod",
    )(c_b, ada_w, ada_b.reshape(depth, 1, n))


def _norm_kernel(*refs, n_add, tok_gate, write_x, side, h_dtype):
    it = iter(refs)
    x_ref = next(it)
    add_refs = [next(it) for _ in range(n_add)]
    tg_refs = [next(it) for _ in range(n_add)] if tok_gate else []
    gv_ref = next(it) if n_add else None
    g_ref, sc_ref, sh_ref = next(it), next(it), next(it)
    if side:
        ws_hi_ref, ws_lo_ref, bs_ref = next(it), next(it), next(it)
    xo_ref = next(it) if write_x else None
    h_ref = next(it)
    s_ref = next(it) if side else None

    x = x_ref[...]
    if n_add:
        if tok_gate:
            upd = sum(a[...].astype(F32) * tg[:, 0:1] for a, tg in zip(add_refs, tg_refs))
        else:
            upd = sum(a[...].astype(F32) for a in add_refs)
        x = x + gv_ref[0] * upd
    if write_x:
        xo_ref[...] = x
    y = x * lax.rsqrt(jnp.mean(x * x, axis=-1, keepdims=True) + EPS)
    h = (y * g_ref[...]) * (1.0 + sc_ref[0]) + sh_ref[0]
    h_ref[...] = h.astype(h_dtype)
    if side:
        h_hi = h.astype(BF16)
        s = _dot(h_hi, ws_hi_ref[...])
        if side == "x3":
            h_lo = (h - h_hi.astype(F32)).astype(BF16)
            s = s + _dot(h_hi, ws_lo_ref[...]) + _dot(h_lo, ws_hi_ref[...])
        s_ref[...] = s + bs_ref[...]


def _norm_call(x, adds, tok_gates, gate_vec, g, sc, sh, side_w, side_b, *, seq, write_x, side,
               h_dtype, add_row_blocks=None):
    t, d = x.shape
    tm = 256
    nsb = seq // tm
    n_add = len(adds)
    tok_gate = bool(tok_gates)
    row = lambda i: (i, 0)
    vec = lambda i: (i // nsb, 0, 0)
    args, specs = [x], [pl.BlockSpec((tm, d), row)]
    for k, a in enumerate(adds):
        off = 0 if add_row_blocks is None else add_row_blocks[k]
        args.append(a)
        specs.append(pl.BlockSpec((tm, d), lambda i, off=off: (i + off, 0)))
    for tg in tok_gates:
        args.append(tg)
        specs.append(pl.BlockSpec((tm, LANES), row))
    if n_add:
        args.append(gate_vec)
        specs.append(pl.BlockSpec((1, 1, d), vec))
    args += [g.reshape(1, d), sc, sh]
    specs += [pl.BlockSpec((1, d), lambda i: (0, 0)), pl.BlockSpec((1, 1, d), vec),
              pl.BlockSpec((1, 1, d), vec)]
    if side:
        w_hi = side_w.astype(BF16)
        w_lo = (side_w - w_hi.astype(F32)).astype(BF16)
        args += [w_hi, w_lo, side_b]
        specs += [pl.BlockSpec((d, LANES), lambda i: (0, 0)), pl.BlockSpec((d, LANES), lambda i: (0, 0)),
                  pl.BlockSpec((1, LANES), lambda i: (0, 0))]
    out_shape, out_specs = [], []
    if write_x:
        out_shape.append(jax.ShapeDtypeStruct((t, d), F32))
        out_specs.append(pl.BlockSpec((tm, d), row))
    out_shape.append(jax.ShapeDtypeStruct((t, d), h_dtype))
    out_specs.append(pl.BlockSpec((tm, d), row))
    if side:
        out_shape.append(jax.ShapeDtypeStruct((t, LANES), F32))
        out_specs.append(pl.BlockSpec((tm, LANES), row))
    return pl.pallas_call(
        functools.partial(_norm_kernel, n_add=n_add, tok_gate=tok_gate, write_x=write_x, side=side,
                          h_dtype=h_dtype),
        grid=(t // tm,), in_specs=specs, out_specs=out_specs, out_shape=out_shape,
        compiler_params=_cparams(1), name="resid_norm",
    )(*args)


def _mm_kernel(a_ref, w_ref, o_ref):
    o_ref[...] = _dot(a_ref[...], w_ref[...]).astype(o_ref.dtype)


def _matmul(a, w, out_dtype, tm=1024, tn=512):
    m, k = a.shape
    n = w.shape[1]
    tm = min(tm, m)
    return pl.pallas_call(
        _mm_kernel, grid=(m // tm, n // tn),
        in_specs=[pl.BlockSpec((tm, k), lambda i, j: (i, 0)), pl.BlockSpec((k, tn), lambda i, j: (0, j))],
        out_specs=pl.BlockSpec((tm, tn), lambda i, j: (i, j)),
        out_shape=jax.ShapeDtypeStruct((m, n), out_dtype),
        compiler_params=_cparams(2), name="in_proj",
    )(a, w)


def _mm3_kernel(a1_ref, a2_ref, a3_ref, w_ref, o_ref):
    k1, k2 = a1_ref.shape[1], a2_ref.shape[1]
    acc = _dot(a1_ref[...], w_ref[0:k1, :])
    acc = acc + _dot(a2_ref[...], w_ref[k1:k1 + k2, :])
    acc = acc + _dot(a3_ref[...], w_ref[k1 + k2:, :])
    o_ref[...] = acc.astype(o_ref.dtype)


def _out_proj(y_ret, y_m, y_f, w, out_dtype, tm=1024, tn=512):
    m = y_ret.shape[0]
    k, n = w.shape
    tm = min(tm, m)
    return pl.pallas_call(
        _mm3_kernel, grid=(m // tm, n // tn),
        in_specs=[pl.BlockSpec((tm, y_ret.shape[1]), lambda i, j: (i, 0)),
                  pl.BlockSpec((tm, y_m.shape[1]), lambda i, j: (i, 0)),
                  pl.BlockSpec((tm, y_f.shape[1]), lambda i, j: (i, 0)),
                  pl.BlockSpec((k, tn), lambda i, j: (0, j))],
        out_specs=pl.BlockSpec((tm, tn), lambda i, j: (i, j)),
        out_shape=jax.ShapeDtypeStruct((m, n), out_dtype),
        compiler_params=_cparams(2), name="out_proj",
    )(y_ret, y_m, y_f, w)


def _ret_kernel(q_ref, k_ref, v_ref, g_ref, cos_ref, sin_ref, decay_ref, zeta_ref, xi_ref, gam_ref,
                gn_ref, o_ref, state_ref, *, ts):
    @pl.when(pl.program_id(1) == 0)
    def _():
        state_ref[...] = jnp.zeros_like(state_ref)

    for c in range(ts // CHUNK):
        rows = slice(c * CHUNK, (c + 1) * CHUNK)
        cosv, sinv = cos_ref[rows, :], sin_ref[rows, :]
        for h in range(RET_HEADS):
            cols = slice(h * HEAD_DIM, (h + 1) * HEAD_DIM)
            q = q_ref[rows, cols].astype(F32)
            k = k_ref[rows, cols].astype(F32)
            qr = q * cosv + pltpu.roll(q, HEAD_DIM // 2, 1) * sinv
            kr = k * cosv + pltpu.roll(k, HEAD_DIM // 2, 1) * sinv
            qb, kb, vb = qr.astype(BF16), kr.astype(BF16), v_ref[rows, cols]
            s = _dot_nt(qb, kb) * decay_ref[h]
            state = state_ref[h]
            y = _dot(s.astype(BF16), vb) + _dot(qb, state.astype(BF16)) * xi_ref[h]
            kz_t = (kr * zeta_ref[h]).T.astype(BF16)
            state_ref[h] = state * gam_ref[h] + _dot(kz_t, vb)
            out = _group_norm(y, gn_ref[:, cols]) * _silu(g_ref[rows, cols].astype(F32))
            o_ref[rows, cols] = out.astype(o_ref.dtype)


def _retention(proj, gn_g, consts, *, batch, seq):
    ts = 512
    ns = seq // ts
    cosf, sins, decay, zeta, xi, gam = consts
    rowblk = lambda cb: pl.BlockSpec((ts, D_RET), lambda b, s, cb=cb: (b * ns + s, cb))
    full3 = pl.BlockSpec((RET_HEADS, CHUNK, HEAD_DIM), lambda b, s: (0, 0, 0))
    return pl.pallas_call(
        functools.partial(_ret_kernel, ts=ts), grid=(batch, ns),
        in_specs=[rowblk(COL_RQ // D_RET), rowblk(COL_RK // D_RET), rowblk(COL_RV // D_RET),
                  rowblk(COL_RG // D_RET),
                  pl.BlockSpec((ts, HEAD_DIM), lambda b, s: (s, 0)),
                  pl.BlockSpec((ts, HEAD_DIM), lambda b, s: (s, 0)),
                  full3, full3, full3,
                  pl.BlockSpec((RET_HEADS, 1, HEAD_DIM), lambda b, s: (0, 0, 0)),
                  pl.BlockSpec((1, D_RET), lambda b, s: (0, 0))],
        out_specs=pl.BlockSpec((ts, D_RET), lambda b, s: (b * ns + s, 0)),
        out_shape=jax.ShapeDtypeStruct((batch * seq, D_RET), BF16),
        scratch_shapes=[pltpu.VMEM((RET_HEADS, HEAD_DIM, HEAD_DIM), F32)],
        compiler_params=_cparams(2), name="retention",
    )(proj, proj, proj, proj, cosf, sins, decay, zeta, xi, gam, gn_g.reshape(1, D_RET))


def _retention_consts(seq):
    h = RET_HEADS
    log_gamma = jnp.log1p(-jnp.power(2.0, -5.0 - jnp.arange(h, dtype=F32)))
    pos = jnp.arange(CHUNK, dtype=F32)
    diff = pos[:, None] - pos[None, :]
    decay = jnp.where(diff >= 0, jnp.exp(jnp.maximum(diff, 0.0)[None] * log_gamma[:, None, None]), 0.0)
    zeta = jnp.exp((CHUNK - 1 - pos)[None, :] * log_gamma[:, None])
    xi = jnp.exp((pos + 1)[None, :] * log_gamma[:, None])
    gam = jnp.exp(CHUNK * log_gamma)
    bcast = lambda v: jnp.broadcast_to(v[:, :, None], (h, CHUNK, HEAD_DIM))
    half = HEAD_DIM // 2
    inv = ROPE_BASE ** (-jnp.arange(half, dtype=F32) / half)
    ang = jnp.arange(seq, dtype=F32)[:, None] * inv[None, :]
    cosf = jnp.concatenate([jnp.cos(ang), jnp.cos(ang)], axis=1)
    sins = jnp.concatenate([-jnp.sin(ang), jnp.sin(ang)], axis=1)
    gam_b = jnp.broadcast_to(gam[:, None, None], (h, 1, HEAD_DIM))
    return cosf, sins, decay, bcast(zeta), bcast(xi), gam_b


def _mlstm_kernel(x_ref, v_ref, og_ref, gates_ref, cw_ref, cb_ref, wq_ref, wk_ref, gn_ref, o_ref,
                  xbuf, c_state, n_state, m_state, *, ts):
    @pl.when(pl.program_id(1) == 0)
    def _():
        c_state[...] = jnp.zeros_like(c_state)
        n_state[...] = jnp.zeros_like(n_state)
        m_state[...] = jnp.full_like(m_state, NEG_INF)
        xbuf[0:8, :] = jnp.zeros((8, D_MLSTM), F32)

    x = x_ref[...].astype(F32)
    xbuf[8:ts + 8, :] = x
    conv = cb_ref[...]
    for j in range(CONV_WIDTH):
        start = 8 - (CONV_WIDTH - 1) + j
        conv = conv + cw_ref[j:j + 1, :] * xbuf[start:start + ts, :]
    xbuf[0:8, :] = x[ts - 8:ts, :]
    xc = _silu(conv).astype(BF16)

    tri = _lower_tri(CHUNK, BF16)
    r_i = lax.broadcasted_iota(jnp.int32, (CHUNK, CHUNK), 0)
    c_i = lax.broadcasted_iota(jnp.int32, (CHUNK, CHUNK), 1)
    causal = c_i <= r_i

    q_all = [_dot(xc[:, h * HEAD_DIM:(h + 1) * HEAD_DIM], wq_ref[h]).astype(BF16) for h in range(MLSTM_HEADS)]
    k_all = [_dot(xc[:, h * HEAD_DIM:(h + 1) * HEAD_DIM], wk_ref[h]) for h in range(MLSTM_HEADS)]

    for c in range(ts // CHUNK):
        rows = slice(c * CHUNK, (c + 1) * CHUNK)
        gc = gates_ref[rows, :]
        gt = gc.T
        b_col = _cumsum_rows(tri, _log_sigmoid(gc))
        b_row = b_col.T
        for h in range(MLSTM_HEADS):
            cols = slice(h * HEAD_DIM, (h + 1) * HEAD_DIM)
            li_r = gt[LANE_MI + h:LANE_MI + h + 1, :]
            li_c = gc[:, LANE_MI + h:LANE_MI + h + 1]
            b_c = b_col[:, LANE_MF + h:LANE_MF + h + 1]
            b_r = b_row[LANE_MF + h:LANE_MF + h + 1, :]
            b_end = b_r[:, CHUNK - 1:CHUNK]
            m_prev = m_state[h][:, 0:1]
            c_prev, n_prev = c_state[h], n_state[h]

            d_log = jnp.where(causal, b_c - b_r + li_r, NEG_INF)
            a_c = b_end - b_c + li_c
            a_max = jnp.max(a_c, axis=0, keepdims=True)
            w_c = jnp.exp(a_c - a_max)
            inter = b_c + m_prev
            m_t = jnp.maximum(jnp.max(d_log, axis=1, keepdims=True), inter)
            d_w = jnp.exp(d_log - m_t)
            inter_w = jnp.exp(inter - m_t)

            q = q_all[h][rows, :]
            kf = k_all[h][rows, :]
            k = kf.astype(BF16)
            v = v_ref[rows, cols]
            s = _dot_nt(q, k) * d_w
            num = _dot(s.astype(BF16), v) + inter_w * _dot(q, c_prev.astype(BF16))
            qn = jnp.sum(q.astype(F32) * n_prev, axis=1, keepdims=True)
            den = jnp.sum(s, axis=1, keepdims=True) + inter_w * qn
            hh = num / jnp.maximum(jnp.abs(den), jnp.exp(-m_t))

            kw = kf * w_c
            kv = _dot(kw.T.astype(BF16), v)
            ks = jnp.sum(kw, axis=0, keepdims=True)
            m_new = jnp.maximum(b_end + m_prev, a_max)
            s_old = jnp.exp(b_end + m_prev - m_new)
            s_new = jnp.exp(a_max - m_new)
            c_state[h] = s_old * c_prev + s_new * kv
            n_state[h] = s_old * n_prev + s_new * ks
            m_state[h] = jnp.broadcast_to(m_new, (1, HEAD_DIM))

            out = jax.nn.sigmoid(og_ref[rows, cols].astype(F32)) * _group_norm(hh, gn_ref[:, cols])
            o_ref[rows, cols] = out.astype(o_ref.dtype)


def _mlstm(proj, gates, conv_w, conv_b, wq, wk, gn_g, *, batch, seq):
    ts = 512
    ns = seq // ts
    rowblk = lambda cb: pl.BlockSpec((ts, D_MLSTM), lambda b, s, cb=cb: (b * ns + s, cb))
    wspec = pl.BlockSpec((MLSTM_HEADS, HEAD_DIM, HEAD_DIM), lambda b, s: (0, 0, 0))
    return pl.pallas_call(
        functools.partial(_mlstm_kernel, ts=ts), grid=(batch, ns),
        in_specs=[rowblk(COL_MX // D_MLSTM), rowblk(COL_MV // D_MLSTM), rowblk(COL_MO // D_MLSTM),
                  pl.BlockSpec((ts, LANES), lambda b, s: (b * ns + s, 0)),
                  pl.BlockSpec((CONV_WIDTH, D_MLSTM), lambda b, s: (0, 0)),
                  pl.BlockSpec((1, D_MLSTM), lambda b, s: (0, 0)),
                  wspec, wspec,
                  pl.BlockSpec((1, D_MLSTM), lambda b, s: (0, 0))],
        out_specs=pl.BlockSpec((ts, D_MLSTM), lambda b, s: (b * ns + s, 0)),
        out_shape=jax.ShapeDtypeStruct((batch * seq, D_MLSTM), BF16),
        scratch_shapes=[pltpu.VMEM((ts + 8, D_MLSTM), F32),
                        pltpu.VMEM((MLSTM_HEADS, HEAD_DIM, HEAD_DIM), F32),
                        pltpu.VMEM((MLSTM_HEADS, 1, HEAD_DIM), F32),
                        pltpu.VMEM((MLSTM_HEADS, 1, HEAD_DIM), F32)],
        compiler_params=_cparams(2), name="mlstm",
    )(proj, proj, proj, gates, conv_w, conv_b.reshape(1, D_MLSTM), wq.astype(BF16),
      (wk * HEAD_DIM ** -0.5).astype(BF16), gn_g.reshape(1, D_MLSTM))


def _fcum_kernel(gates_ref, fcol_ref, frow_ref, carry_ref, *, ts):
    @pl.when(pl.program_id(1) == 0)
    def _():
        carry_ref[...] = jnp.zeros_like(carry_ref)

    tri = _lower_tri(CHUNK, BF16)
    carry = carry_ref[...]
    for c in range(ts // CHUNK):
        rows = slice(c * CHUNK, (c + 1) * CHUNK)
        cum = _cumsum_rows(tri, _log_sigmoid(gates_ref[rows, :])) + carry
        carry = cum[CHUNK - 1:CHUNK, :]
        fcol_ref[rows, :] = cum
        frow_ref[0, :, rows] = cum.T[LANE_FF:LANE_FF + FOX_HEADS, :]
    carry_ref[...] = carry


def _forget_cumsum(gates, *, batch, seq):
    ts = 512
    ns = seq // ts
    return pl.pallas_call(
        functools.partial(_fcum_kernel, ts=ts), grid=(batch, ns),
        in_specs=[pl.BlockSpec((ts, LANES), lambda b, s: (b * ns + s, 0))],
        out_specs=[pl.BlockSpec((ts, LANES), lambda b, s: (b * ns + s, 0)),
                   pl.BlockSpec((1, FOX_HEADS, ts), lambda b, s: (b, 0, s))],
        out_shape=[jax.ShapeDtypeStruct((batch * seq, LANES), F32),
                   jax.ShapeDtypeStruct((batch, FOX_HEADS, seq), F32)],
        scratch_shapes=[pltpu.VMEM((1, LANES), F32)],
        compiler_params=_cparams(2), name="forget_cumsum",
    )(gates)


def _fox_kernel(q_ref, k_ref, v_ref, fcol_ref, frow_ref, o_ref, acc_ref, m_ref, l_ref, *, tq):
    h = pl.program_id(1)
    i = pl.program_id(2)
    q = q_ref[...]
    lane = lax.broadcasted_iota(jnp.int32, (tq, LANES), 1)
    f_q = jnp.sum(jnp.where(lane == LANE_FF + h, fcol_ref[...], 0.0), axis=1, keepdims=True)
    acc_ref[...] = jnp.zeros_like(acc_ref)
    m_ref[...] = jnp.full_like(m_ref, NEG_INF)
    l_ref[...] = jnp.zeros_like(l_ref)

    def step(j, masked):
        start = pl.multiple_of(j * tq, tq)
        kb = k_ref[pl.ds(start, tq), :]
        vb = v_ref[pl.ds(start, tq), :]
        s = _dot_nt(q, kb) - frow_ref[0, :, pl.ds(start, tq)]
        if masked:
            r_i = lax.broadcasted_iota(jnp.int32, (tq, tq), 0)
            c_i = lax.broadcasted_iota(jnp.int32, (tq, tq), 1)
            s = jnp.where(c_i <= r_i, s, NEG_INF)
        m_old = m_ref[...]
        m_new = jnp.maximum(m_old, jnp.max(s, axis=1, keepdims=True) + f_q)
        alpha = jnp.exp(m_old - m_new)
        p = jnp.exp(s + (f_q - m_new))
        l_ref[...] = alpha * l_ref[...] + jnp.sum(p, axis=1, keepdims=True)
        acc_ref[...] = alpha * acc_ref[...] + _dot(p.astype(BF16), vb)
        m_ref[...] = m_new

    def body(j, carry):
        step(j, False)
        return carry

    lax.fori_loop(0, i, body, 0)
    step(i, True)
    o_ref[...] = (acc_ref[...] / l_ref[...]).astype(o_ref.dtype)


def _fox_attention(proj, fcol, frow, *, batch, seq):
    tq = 512
    nq = seq // tq
    cq, ck, cv = COL_FQ // HEAD_DIM, COL_FK // HEAD_DIM, COL_FV // HEAD_DIM
    return pl.pallas_call(
        functools.partial(_fox_kernel, tq=tq), grid=(batch, FOX_HEADS, nq),
        in_specs=[pl.BlockSpec((tq, HEAD_DIM), lambda b, h, i: (b * nq + i, cq + h)),
                  pl.BlockSpec((seq, HEAD_DIM), lambda b, h, i: (b, ck + h)),
                  pl.BlockSpec((seq, HEAD_DIM), lambda b, h, i: (b, cv + h)),
                  pl.BlockSpec((tq, LANES), lambda b, h, i: (b * nq + i, 0)),
                  pl.BlockSpec((1, 1, seq), lambda b, h, i: (b * FOX_HEADS + h, 0, 0))],
        out_specs=pl.BlockSpec((tq, HEAD_DIM), lambda b, h, i: (b * nq + i, h)),
        out_shape=jax.ShapeDtypeStruct((batch * seq, D_FOX), BF16),
        scratch_shapes=[pltpu.VMEM((tq, HEAD_DIM), F32), pltpu.VMEM((tq, 1), F32),
                        pltpu.VMEM((tq, 1), F32)],
        compiler_params=_cparams(3), name="fox_attention",
    )(proj, proj, proj, fcol, frow.reshape(batch * FOX_HEADS, 1, seq))


def _route_kernel(logit_ref, eid_ref, gate_ref, rank_ref, cnt_ref, upper_ref, base_ref, *, tm):
    @pl.when(pl.program_id(0) == 0)
    def _():
        r = lax.broadcasted_iota(jnp.int32, (tm, tm), 0)
        c = lax.broadcasted_iota(jnp.int32, (tm, tm), 1)
        upper_ref[...] = jnp.where(r < c, 1.0, 0.0).astype(BF16)
        base_ref[...] = jnp.zeros_like(base_ref)

    lt = logit_ref[...].T
    e = lt[LANE_RE:LANE_RE + N_EXPERTS, :]
    g = lt[LANE_RG:LANE_RG + N_GROUPS, :]
    g_max = jnp.max(g, axis=0, keepdims=True)
    g_val = 1.0 / jnp.sum(jnp.exp(g - g_max), axis=0, keepdims=True)
    g_row = lax.broadcasted_iota(jnp.int32, (N_GROUPS, tm), 0).astype(F32)
    g_idx = jnp.min(jnp.where(g == g_max, g_row, float(N_GROUPS)), axis=0, keepdims=True)

    e_row_i = lax.broadcasted_iota(jnp.int32, (N_EXPERTS, tm), 0)
    e_row = e_row_i.astype(F32)
    e_grp = jnp.right_shift(e_row_i, EXPERTS_PER_GROUP.bit_length() - 1).astype(F32)
    el = jnp.where(e_grp == g_idx, e, NEG_INF)
    max1 = jnp.max(el, axis=0, keepdims=True)
    idx1 = jnp.min(jnp.where(el == max1, e_row, float(N_EXPERTS)), axis=0, keepdims=True)
    el2 = jnp.where(e_row == idx1, NEG_INF, el)
    max2 = jnp.max(el2, axis=0, keepdims=True)
    idx2 = jnp.min(jnp.where(el2 == max2, e_row, float(N_EXPERTS)), axis=0, keepdims=True)
    e_sum = jnp.sum(jnp.exp(el - max1), axis=0, keepdims=True)
    p1 = 1.0 / e_sum
    p2 = jnp.exp(max2 - max1) / e_sum
    p_sum = p1 + p2
    gate_ref[0:1, :] = g_val * (p1 / p_sum)
    gate_ref[1:2, :] = g_val * (p2 / p_sum)
    eid_ref[0:1, :] = idx1.astype(jnp.int32)
    eid_ref[1:2, :] = idx2.astype(jnp.int32)

    oh1 = e_row == idx1
    oh2 = e_row == idx2
    onehot = jnp.where(oh1 | oh2, 1.0, 0.0)
    before = _dot(onehot.astype(BF16), upper_ref[...]) + base_ref[:, 0:1]
    rank_ref[0:1, :] = jnp.sum(jnp.where(oh1, before, 0.0), axis=0, keepdims=True).astype(jnp.int32)
    rank_ref[1:2, :] = jnp.sum(jnp.where(oh2, before, 0.0), axis=0, keepdims=True).astype(jnp.int32)
    base_ref[...] = base_ref[...] + jnp.sum(onehot, axis=1, keepdims=True)
    cnt_ref[...] = base_ref[...]


def _route(logits):
    t = logits.shape[0]
    tm = min(1024, t)
    tok = lambda i: (0, i)
    return pl.pallas_call(
        functools.partial(_route_kernel, tm=tm), grid=(t // tm,),
        in_specs=[pl.BlockSpec((tm, LANES), lambda i: (i, 0))],
        out_specs=[pl.BlockSpec((2, tm), tok), pl.BlockSpec((2, tm), tok), pl.BlockSpec((2, tm), tok),
                   pl.BlockSpec((N_EXPERTS, LANES), lambda i: (0, 0))],
        out_shape=[jax.ShapeDtypeStruct((2, t), jnp.int32), jax.ShapeDtypeStruct((2, t), F32),
                   jax.ShapeDtypeStruct((2, t), jnp.int32),
                   jax.ShapeDtypeStruct((N_EXPERTS, LANES), F32)],
        scratch_shapes=[pltpu.VMEM((tm, tm), BF16), pltpu.VMEM((N_EXPERTS, LANES), F32)],
        compiler_params=_cparams(1), name="route",
    )(logits)


def _moe_kernel(tile_expert, tile_valid, row_id, h_hbm, w1_ref, w3_ref, w2_ref, o_hbm,
                xbuf, ybuf, gsem, ssem, *, tm, n_tokens, n_tiles):
    del tile_expert
    i = pl.program_id(0)
    nxt = jnp.minimum(i + 1, n_tiles - 1)
    has_next = jnp.logical_and(i + 1 < n_tiles, tile_valid[nxt] > 0)

    def gather_rows(tile, slot):
        base = tile * tm

        def body(r, carry):
            rid = row_id[base + r]
            tok = jnp.where(rid >= n_tokens, rid - n_tokens, rid)
            pltpu.make_async_copy(h_hbm.at[pl.ds(tok, 1)], xbuf.at[slot, pl.ds(r, 1)],
                                  gsem.at[slot]).start()
            return carry

        lax.fori_loop(0, tm, body, 0, unroll=8)

    def wait_gather(slot):
        pltpu.make_async_copy(h_hbm.at[pl.ds(0, tm)], xbuf.at[slot], gsem.at[slot]).wait()

    def scatter_row(base, r):
        pltpu.make_async_copy(ybuf.at[pl.ds(r, 1)], o_hbm.at[pl.ds(row_id[base + r], 1)], ssem).start()

    def scatter_rows(tile):
        base = tile * tm
        nv = tile_valid[tile]
        n_full = nv // 8

        def group(g, carry):
            for u in range(8):
                scatter_row(base, g * 8 + u)
            return carry

        lax.fori_loop(0, n_full, group, 0)
        for u in range(7):
            @pl.when(n_full * 8 + u < nv)
            def _():
                scatter_row(base, n_full * 8 + u)

    def wait_scatter(tile):
        nv = tile_valid[tile]
        n_full = nv // 8

        @pl.when(n_full > 0)
        def _():
            n8 = pl.multiple_of(n_full * 8, 8)
            pltpu.make_async_copy(ybuf.at[pl.ds(0, n8)], o_hbm.at[pl.ds(0, n8)], ssem).wait()

        for u in range(7):
            @pl.when(n_full * 8 + u < nv)
            def _():
                pltpu.make_async_copy(ybuf.at[pl.ds(0, 1)], o_hbm.at[pl.ds(0, 1)], ssem).wait()

    @pl.when(tile_valid[i] > 0)
    def _():
        slot = lax.rem(i, 2)

        @pl.when(i == 0)
        def _():
            gather_rows(0, 0)

        wait_gather(slot)

        @pl.when(has_next)
        def _():
            gather_rows(i + 1, 1 - slot)

        x = xbuf[slot].astype(BF16)
        h1 = _dot(x, w1_ref[0].astype(BF16))
        h3 = _dot(x, w3_ref[0].astype(BF16))
        act = (_silu(h1) * h3).astype(BF16)
        y = _dot(act, w2_ref[0].astype(BF16))

        @pl.when(i > 0)
        def _():
            wait_scatter(i - 1)

        ybuf[...] = y
        scatter_rows(i)

        @pl.when(jnp.logical_not(has_next))
        def _():
            wait_scatter(i)


def _moe_experts(h2, w1, w3, w2, tile_expert, tile_valid, row_id, *, n_tiles):
    t, d = h2.shape
    de = w1.shape[2]
    tm = MOE_TILE
    wmap = lambda i, te, tv, rid: (te[i], 0, 0)
    grid_spec = pltpu.PrefetchScalarGridSpec(
        num_scalar_prefetch=3, grid=(n_tiles,),
        in_specs=[pl.BlockSpec(memory_space=pl.ANY),
                  pl.BlockSpec((1, d, de), wmap), pl.BlockSpec((1, d, de), wmap),
                  pl.BlockSpec((1, de, d), wmap)],
        out_specs=pl.BlockSpec(memory_space=pl.ANY),
        scratch_shapes=[pltpu.VMEM((2, tm, d), F32), pltpu.VMEM((tm, d), F32),
                        pltpu.SemaphoreType.DMA((2,)), pltpu.SemaphoreType.DMA(())])
    return pl.pallas_call(
        functools.partial(_moe_kernel, tm=tm, n_tokens=t, n_tiles=n_tiles), grid_spec=grid_spec,
        out_shape=jax.ShapeDtypeStruct((2 * t, d), F32),
        compiler_params=_cparams(1), name="moe_experts",
    )(tile_expert, tile_valid, row_id, h2, w1, w3, w2)


def _dispatch_plan(eid, rank, counts, *, n_tokens, n_tiles):
    tm = MOE_TILE
    cnt = counts[:, 0].astype(jnp.int32)
    tiles_e = (cnt + tm - 1) // tm
    tile_end = jnp.cumsum(tiles_e)
    tile_start = tile_end - tiles_e
    dest = (tile_start * tm)[eid] + rank
    n_active = tile_end[-1]
    tile_ids = jnp.arange(n_tiles, dtype=jnp.int32)
    tile_expert = jnp.searchsorted(tile_end, jnp.minimum(tile_ids, n_active - 1), side="right")
    tile_expert = jnp.minimum(tile_expert, N_EXPERTS - 1).astype(jnp.int32)
    in_expert = tile_ids - tile_start[tile_expert]
    tile_valid = jnp.clip(cnt[tile_expert] - in_expert * tm, 0, tm)
    tile_valid = jnp.where(tile_ids < n_active, tile_valid, 0).astype(jnp.int32)
    slot_row = (jnp.arange(n_tokens, dtype=jnp.int32)[None, :]
                + jnp.arange(2, dtype=jnp.int32)[:, None] * n_tokens)
    row_id = jnp.zeros((n_tiles * tm,), jnp.int32).at[dest.reshape(-1)].set(slot_row.reshape(-1))
    return tile_expert, tile_valid, row_id


def _rearranged_w_in(w_in):
    s = HEAD_DIM ** -0.5
    g0 = COL_MO + D_MLSTM
    f0 = g0 + 2 * MLSTM_HEADS
    g1 = f0 + 3 * D_FOX
    big = jnp.concatenate([w_in[..., 0:COL_RK], w_in[..., COL_RK:COL_RV] * s, w_in[..., COL_RV:g0],
                           w_in[..., f0:f0 + D_FOX] * s, w_in[..., f0 + D_FOX:g1]], axis=-1)
    gate = jnp.concatenate([w_in[..., g0:f0], w_in[..., g1:g1 + FOX_HEADS]], axis=-1)
    gate = jnp.pad(gate, ((0, 0), (0, 0), (0, LANES - gate.shape[-1])))
    return big.astype(BF16), gate


def _lane_pad(v):
    return jnp.pad(v, ((0, 0), (0, LANES - v.shape[-1])))


def kernel(x, c, ada_w, ada_b, norm1_g, w_in, ret_gn_g, mlstm_conv_w, mlstm_conv_b, mlstm_wq, mlstm_wk,
           mlstm_i_b, mlstm_f_b, mlstm_gn_g, fox_f_b, w_out, norm2_g, router_group_w, router_group_b,
           router_expert_w, router_expert_b, moe_w1, moe_w3, moe_w2, final_g):
    batch, seq, d = x.shape
    depth = ada_w.shape[0]
    t = batch * seq
    n_tiles = (2 * t) // MOE_TILE + N_EXPERTS
    blocks_per_slot = t // 256

    mod = _modulation(c, ada_w, ada_b)
    mod = mod.reshape(depth, batch, N_MOD, 1, d)
    w_big, w_gate = _rearranged_w_in(w_in)
    gate_bias = _lane_pad(jnp.concatenate([mlstm_i_b, mlstm_f_b, fox_f_b], axis=-1))
    w_route = jnp.concatenate([router_expert_w, router_group_w], axis=-1)
    w_route = jnp.pad(w_route, ((0, 0), (0, 0), (0, LANES - w_route.shape[-1])))
    route_bias = _lane_pad(jnp.concatenate([router_expert_b, router_group_b], axis=-1))
    w_out_b = w_out.astype(BF16)
    ret_consts = _retention_consts(seq)

    xf = x.reshape(t, d)
    moe_out, tok_gates, g2_prev = None, (), None
    for l in range(depth):
        sh1, sc1, g1, sh2, sc2, g2 = (mod[l, :, k] for k in range(N_MOD))
        if l == 0:
            h1, gates = _norm_call(xf, (), (), None, norm1_g[l], sc1, sh1, w_gate[l], gate_bias[l:l + 1],
                                   seq=seq, write_x=False, side="x1", h_dtype=BF16)
        else:
            xf, h1, gates = _norm_call(xf, (moe_out, moe_out), tok_gates, g2_prev, norm1_g[l], sc1, sh1,
                                       w_gate[l], gate_bias[l:l + 1], seq=seq, write_x=True, side="x1",
                                       h_dtype=BF16, add_row_blocks=(0, blocks_per_slot))
        proj = _matmul(h1, w_big[l], BF16)
        y_ret = _retention(proj, ret_gn_g[l], ret_consts, batch=batch, seq=seq)
        y_m = _mlstm(proj, gates, mlstm_conv_w[l], mlstm_conv_b[l], mlstm_wq[l], mlstm_wk[l],
                     mlstm_gn_g[l], batch=batch, seq=seq)
        fcol, frow = _forget_cumsum(gates, batch=batch, seq=seq)
        y_f = _fox_attention(proj, fcol, frow, batch=batch, seq=seq)
        mix = _out_proj(y_ret, y_m, y_f, w_out_b[l], BF16)
        xf, h2, logits = _norm_call(xf, (mix,), (), g1, norm2_g[l], sc2, sh2, w_route[l],
                                    route_bias[l:l + 1], seq=seq, write_x=True, side="x3", h_dtype=F32)
        eid, gate, rank, counts = _route(logits)
        tile_expert, tile_valid, row_id = _dispatch_plan(eid, rank, counts, n_tokens=t, n_tiles=n_tiles)
        moe_out = _moe_experts(h2, moe_w1[l], moe_w3[l], moe_w2[l], tile_expert, tile_valid, row_id,
                               n_tiles=n_tiles)
        tok_gates = tuple(jnp.broadcast_to(gate[k][:, None], (t, LANES)) for k in range(2))
        g2_prev = g2
    zeros = jnp.zeros((batch, 1, d), F32)
    (out,) = _norm_call(xf, (moe_out, moe_out), tok_gates, g2_prev, final_g, zeros, zeros, None, None,
                        seq=seq, write_x=False, side=None, h_dtype=F32,
                        add_row_blocks=(0, blocks_per_slot))
    return out.reshape(batch, seq, d)
```

```python
import functools

import jax
import jax.numpy as jnp
from jax import lax
from jax.experimental import pallas as pl
from jax.experimental.pallas import tpu as pltpu

F32 = jnp.float32
BF16 = jnp.bfloat16

HEAD_DIM = 128
LANES = 128
RET_HEADS = 4
MLSTM_HEADS = 4
FOX_HEADS = 8
D_RET = RET_HEADS * HEAD_DIM
D_MLSTM = MLSTM_HEADS * HEAD_DIM
D_FOX = FOX_HEADS * HEAD_DIM
CHUNK = 128
CONV_WIDTH = 4
ROPE_BASE = 10000.0
N_GROUPS = 4
EXPERTS_PER_GROUP = 8
N_EXPERTS = N_GROUPS * EXPERTS_PER_GROUP
N_MOD = 6
EPS = 1e-6
NEG_INF = float("-inf")

COL_RQ, COL_RK, COL_RV, COL_RG = 0, 512, 1024, 1536
COL_MX, COL_MV, COL_MO = 2048, 2560, 3072
COL_FQ, COL_FK, COL_FV = 3584, 4608, 5632
N_PROJ = 6656
LANE_MI, LANE_MF, LANE_FF = 0, 4, 8
LANE_RE, LANE_RG = 0, 32

VMEM_LIMIT = 56 * 1024 * 1024
MOE_TILE = 512


def _cparams(n_grid):
    return pltpu.CompilerParams(dimension_semantics=("arbitrary",) * n_grid,
                                vmem_limit_bytes=VMEM_LIMIT)


def _dot(a, b):
    return jnp.dot(a, b, preferred_element_type=F32)


def _dot_nt(a, b):
    return lax.dot_general(a, b, (((1,), (1,)), ((), ())), preferred_element_type=F32)


def _silu(x):
    return x * jax.nn.sigmoid(x)


def _log_sigmoid(x):
    return jnp.minimum(x, 0.0) - jnp.log(1.0 + jnp.exp(-jnp.abs(x)))


def _split3(x):
    hi = x.astype(BF16)
    r1 = x - hi.astype(F32)
    mid = r1.astype(BF16)
    lo = (r1 - mid.astype(F32)).astype(BF16)
    return hi, mid, lo


def _cumsum_rows(tri, x):
    hi, mid, lo = _split3(x)
    return _dot(tri, hi) + _dot(tri, mid) + _dot(tri, lo)


def _lower_tri(n, dtype):
    r = lax.broadcasted_iota(jnp.int32, (n, n), 0)
    c = lax.broadcasted_iota(jnp.int32, (n, n), 1)
    return jnp.where(c <= r, 1.0, 0.0).astype(dtype)


def _group_norm(y, gain):
    mu = jnp.mean(y, axis=-1, keepdims=True)
    yc = y - mu
    var = jnp.mean(yc * yc, axis=-1, keepdims=True)
    return yc * lax.rsqrt(var + EPS) * gain


def _mod_kernel(c_ref, w_ref, b_ref, o_ref, *, nb, tn):
    w = w_ref[0]
    rows = []
    for b in range(nb):
        cb = _silu(c_ref[b])
        parts = [jnp.sum(w[:, j * LANES:(j + 1) * LANES] * cb, axis=0, keepdims=True)
                 for j in range(tn // LANES)]
        rows.append(jnp.concatenate(parts, axis=1))
    o_ref[0] = jnp.concatenate(rows, axis=0) + b_ref[0]


def _modulation(c, ada_w, ada_b):
    depth, d, n = ada_w.shape
    nb = c.shape[0]
    tn = 512
    c_b = jnp.broadcast_to(c[:, :, None], (nb, d, LANES))
    return pl.pallas_call(
        functools.partial(_mod_kernel, nb=nb, tn=tn),
        grid=(depth, n // tn),
        in_specs=[pl.BlockSpec((nb, d, LANES), lambda l, j: (0, 0, 0)),
                  pl.BlockSpec((1, d, tn), lambda l, j: (l, 0, j)),
                  pl.BlockSpec((1, 1, tn), lambda l, j: (l, 0, j))],
        out_specs=pl.BlockSpec((1, nb, tn), lambda l, j: (l, 0, j)),
        out_shape=jax.ShapeDtypeStruct((depth, nb, n), F32),
        compiler_params=_cparams(2),
        name="adaln_mod",
    )(c_b, ada_w, ada_b.reshape(depth, 1, n))


def _norm_kernel(*refs, n_add, tok_gate, write_x, side, h_dtype):
    it = iter(refs)
    x_ref = next(it)
    add_refs = [next(it) for _ in range(n_add)]
    tg_refs = [next(it) for _ in range(n_add)] if tok_gate else []
    gv_ref = next(it) if n_add else None
    g_ref, sc_ref, sh_ref = next(it), next(it), next(it)
    if side:
        ws_hi_ref, ws_lo_ref, bs_ref = next(it), next(it), next(it)
    xo_ref = next(it) if write_x else None
    h_ref = next(it)
    s_ref = next(it) if side else None

    x = x_ref[...]
    if n_add:
        if tok_gate:
            upd = sum(a[...].astype(F32) * tg[:, 0:1] for a, tg in zip(add_refs, tg_refs))
        else:
            upd = sum(a[...].astype(F32) for a in add_refs)
        x = x + gv_ref[0] * upd
    if write_x:
        xo_ref[...] = x
    y = x * lax.rsqrt(jnp.mean(x * x, axis=-1, keepdims=True) + EPS)
    h = (y * g_ref[...]) * (1.0 + sc_ref[0]) + sh_ref[0]
    h_ref[...] = h.astype(h_dtype)
    if side:
        h_hi = h.astype(BF16)
        s = _dot(h_hi, ws_hi_ref[...])
        if side == "x3":
            h_lo = (h - h_hi.astype(F32)).astype(BF16)
            s = s + _dot(h_hi, ws_lo_ref[...]) + _dot(h_lo, ws_hi_ref[...])
        s_ref[...] = s + bs_ref[...]


def _norm_call(x, adds, tok_gates, gate_vec, g, sc, sh, side_w, side_b, *, seq, write_x, side,
               h_dtype, add_row_offsets=None):
    t, d = x.shape
    tm = 256
    nsb = seq // tm
    n_add = len(adds)
    tok_gate = bool(tok_gates)
    row = lambda i: (i, 0)
    vec = lambda i: (i // nsb, 0, 0)
    args, specs = [x], [pl.BlockSpec((tm, d), row)]
    for k, a in enumerate(adds):
        off = 0 if add_row_offsets is None else add_row_offsets[k] // tm
        args.append(a)
        specs.append(pl.BlockSpec((tm, d), lambda i, off=off: (i + off, 0)))
    for tg in tok_gates:
        args.append(tg)
        specs.append(pl.BlockSpec((tm, LANES), row))
    if n_add:
        args.append(gate_vec)
        specs.append(pl.BlockSpec((1, 1, d), vec))
    args += [g.reshape(1, d), sc, sh]
    specs += [pl.BlockSpec((1, d), lambda i: (0, 0)), pl.BlockSpec((1, 1, d), vec),
              pl.BlockSpec((1, 1, d), vec)]
    if side:
        w_hi = side_w.astype(BF16)
        w_lo = (side_w - w_hi.astype(F32)).astype(BF16)
        args += [w_hi, w_lo, side_b]
        specs += [pl.BlockSpec((d, LANES), lambda i: (0, 0)), pl.BlockSpec((d, LANES), lambda i: (0, 0)),
                  pl.BlockSpec((1, LANES), lambda i: (0, 0))]
    out_shape, out_specs = [], []
    if write_x:
        out_shape.append(jax.ShapeDtypeStruct((t, d), F32))
        out_specs.append(pl.BlockSpec((tm, d), row))
    out_shape.append(jax.ShapeDtypeStruct((t, d), h_dtype))
    out_specs.append(pl.BlockSpec((tm, d), row))
    if side:
        out_shape.append(jax.ShapeDtypeStruct((t, LANES), F32))
        out_specs.append(pl.BlockSpec((tm, LANES), row))
    return pl.pallas_call(
        functools.partial(_norm_kernel, n_add=n_add, tok_gate=tok_gate, write_x=write_x, side=side,
                          h_dtype=h_dtype),
        grid=(t // tm,), in_specs=specs, out_specs=out_specs, out_shape=out_shape,
        compiler_params=_cparams(1), name="resid_norm",
    )(*args)


def _mm_kernel(a_ref, w_ref, o_ref):
    o_ref[...] = _dot(a_ref[...], w_ref[...]).astype(o_ref.dtype)


def _matmul(a, w, out_dtype, tm=1024, tn=512):
    m, k = a.shape
    n = w.shape[1]
    tm = min(tm, m)
    return pl.pallas_call(
        _mm_kernel, grid=(m // tm, n // tn),
        in_specs=[pl.BlockSpec((tm, k), lambda i, j: (i, 0)), pl.BlockSpec((k, tn), lambda i, j: (0, j))],
        out_specs=pl.BlockSpec((tm, tn), lambda i, j: (i, j)),
        out_shape=jax.ShapeDtypeStruct((m, n), out_dtype),
        compiler_params=_cparams(2), name="in_proj",
    )(a, w)


def _mm3_kernel(a1_ref, a2_ref, a3_ref, w_ref, o_ref):
    k1, k2 = a1_ref.shape[1], a2_ref.shape[1]
    acc = _dot(a1_ref[...], w_ref[0:k1, :])
    acc = acc + _dot(a2_ref[...], w_ref[k1:k1 + k2, :])
    acc = acc + _dot(a3_ref[...], w_ref[k1 + k2:, :])
    o_ref[...] = acc.astype(o_ref.dtype)


def _out_proj(y_ret, y_m, y_f, w, out_dtype, tm=1024, tn=512):
    m = y_ret.shape[0]
    k, n = w.shape
    tm = min(tm, m)
    return pl.pallas_call(
        _mm3_kernel, grid=(m // tm, n // tn),
        in_specs=[pl.BlockSpec((tm, y_ret.shape[1]), lambda i, j: (i, 0)),
                  pl.BlockSpec((tm, y_m.shape[1]), lambda i, j: (i, 0)),
                  pl.BlockSpec((tm, y_f.shape[1]), lambda i, j: (i, 0)),
                  pl.BlockSpec((k, tn), lambda i, j: (0, j))],
        out_specs=pl.BlockSpec((tm, tn), lambda i, j: (i, j)),
        out_shape=jax.ShapeDtypeStruct((m, n), out_dtype),
        compiler_params=_cparams(2), name="out_proj",
    )(y_ret, y_m, y_f, w)


def _ret_kernel(q_ref, k_ref, v_ref, g_ref, cos_ref, sin_ref, decay_ref, zeta_ref, xi_ref, gam_ref,
                gn_ref, o_ref, state_ref, *, ts):
    @pl.when(pl.program_id(1) == 0)
    def _():
        state_ref[...] = jnp.zeros_like(state_ref)

    for c in range(ts // CHUNK):
        rows = slice(c * CHUNK, (c + 1) * CHUNK)
        cosv, sinv = cos_ref[rows, :], sin_ref[rows, :]
        for h in range(RET_HEADS):
            cols = slice(h * HEAD_DIM, (h + 1) * HEAD_DIM)
            q = q_ref[rows, cols].astype(F32)
            k = k_ref[rows, cols].astype(F32)
            qr = q * cosv + pltpu.roll(q, HEAD_DIM // 2, 1) * sinv
            kr = k * cosv + pltpu.roll(k, HEAD_DIM // 2, 1) * sinv
            qb, kb, vb = qr.astype(BF16), kr.astype(BF16), v_ref[rows, cols]
            s = _dot_nt(qb, kb) * decay_ref[h]
            state = state_ref[h]
            y = _dot(s.astype(BF16), vb) + _dot(qb, state.astype(BF16)) * xi_ref[h]
            kz_t = (kr * zeta_ref[h]).T.astype(BF16)
            state_ref[h] = state * gam_ref[h] + _dot(kz_t, vb)
            out = _group_norm(y, gn_ref[:, cols]) * _silu(g_ref[rows, cols].astype(F32))
            o_ref[rows, cols] = out.astype(o_ref.dtype)


def _retention(proj, gn_g, consts, *, batch, seq):
    ts = 512
    ns = seq // ts
    cosf, sins, decay, zeta, xi, gam = consts
    rowblk = lambda cb: pl.BlockSpec((ts, D_RET), lambda b, s, cb=cb: (b * ns + s, cb))
    full3 = pl.BlockSpec((RET_HEADS, CHUNK, HEAD_DIM), lambda b, s: (0, 0, 0))
    return pl.pallas_call(
        functools.partial(_ret_kernel, ts=ts), grid=(batch, ns),
        in_specs=[rowblk(COL_RQ // D_RET), rowblk(COL_RK // D_RET), rowblk(COL_RV // D_RET),
                  rowblk(COL_RG // D_RET),
                  pl.BlockSpec((ts, HEAD_DIM), lambda b, s: (s, 0)),
                  pl.BlockSpec((ts, HEAD_DIM), lambda b, s: (s, 0)),
                  full3, full3, full3,
                  pl.BlockSpec((RET_HEADS, 1, HEAD_DIM), lambda b, s: (0, 0, 0)),
                  pl.BlockSpec((1, D_RET), lambda b, s: (0, 0))],
        out_specs=pl.BlockSpec((ts, D_RET), lambda b, s: (b * ns + s, 0)),
        out_shape=jax.ShapeDtypeStruct((batch * seq, D_RET), BF16),
        scratch_shapes=[pltpu.VMEM((RET_HEADS, HEAD_DIM, HEAD_DIM), F32)],
        compiler_params=_cparams(2), name="retention",
    )(proj, proj, proj, proj, cosf, sins, decay, zeta, xi, gam, gn_g.reshape(1, D_RET))


def _retention_consts(seq):
    h = RET_HEADS
    log_gamma = jnp.log1p(-jnp.power(2.0, -5.0 - jnp.arange(h, dtype=F32)))
    pos = jnp.arange(CHUNK, dtype=F32)
    diff = pos[:, None] - pos[None, :]
    decay = jnp.where(diff >= 0, jnp.exp(jnp.maximum(diff, 0.0)[None] * log_gamma[:, None, None]), 0.0)
    zeta = jnp.exp((CHUNK - 1 - pos)[None, :] * log_gamma[:, None])
    xi = jnp.exp((pos + 1)[None, :] * log_gamma[:, None])
    gam = jnp.exp(CHUNK * log_gamma)
    bcast = lambda v: jnp.broadcast_to(v[:, :, None], (h, CHUNK, HEAD_DIM))
    half = HEAD_DIM // 2
    inv = ROPE_BASE ** (-jnp.arange(half, dtype=F32) / half)
    ang = jnp.arange(seq, dtype=F32)[:, None] * inv[None, :]
    cosf = jnp.concatenate([jnp.cos(ang), jnp.cos(ang)], axis=1)
    sins = jnp.concatenate([-jnp.sin(ang), jnp.sin(ang)], axis=1)
    gam_b = jnp.broadcast_to(gam[:, None, None], (h, 1, HEAD_DIM))
    return cosf, sins, decay, bcast(zeta), bcast(xi), gam_b


def _mlstm_kernel(x_ref, v_ref, og_ref, gates_ref, cw_ref, cb_ref, wq_ref, wk_ref, gn_ref, o_ref,
                  xbuf, c_state, n_state, m_state, *, ts):
    @pl.when(pl.program_id(1) == 0)
    def _():
        c_state[...] = jnp.zeros_like(c_state)
        n_state[...] = jnp.zeros_like(n_state)
        m_state[...] = jnp.full_like(m_state, NEG_INF)
        xbuf[0:8, :] = jnp.zeros((8, D_MLSTM), F32)

    x = x_ref[...].astype(F32)
    xbuf[8:ts + 8, :] = x
    conv = cb_ref[...]
    for j in range(CONV_WIDTH):
        start = 8 - (CONV_WIDTH - 1) + j
        conv = conv + cw_ref[j:j + 1, :] * xbuf[start:start + ts, :]
    xbuf[0:8, :] = x[ts - 8:ts, :]
    xc = _silu(conv).astype(BF16)

    tri = _lower_tri(CHUNK, BF16)
    r_i = lax.broadcasted_iota(jnp.int32, (CHUNK, CHUNK), 0)
    c_i = lax.broadcasted_iota(jnp.int32, (CHUNK, CHUNK), 1)
    causal = c_i <= r_i

    q_all = [_dot(xc[:, h * HEAD_DIM:(h + 1) * HEAD_DIM], wq_ref[h]).astype(BF16) for h in range(MLSTM_HEADS)]
    k_all = [_dot(xc[:, h * HEAD_DIM:(h + 1) * HEAD_DIM], wk_ref[h]) for h in range(MLSTM_HEADS)]

    for c in range(ts // CHUNK):
        rows = slice(c * CHUNK, (c + 1) * CHUNK)
        gc = gates_ref[rows, :]
        gt = gc.T
        b_col = _cumsum_rows(tri, _log_sigmoid(gc))
        b_row = b_col.T
        for h in range(MLSTM_HEADS):
            cols = slice(h * HEAD_DIM, (h + 1) * HEAD_DIM)
            li_r = gt[LANE_MI + h:LANE_MI + h + 1, :]
            li_c = gc[:, LANE_MI + h:LANE_MI + h + 1]
            b_c = b_col[:, LANE_MF + h:LANE_MF + h + 1]
            b_r = b_row[LANE_MF + h:LANE_MF + h + 1, :]
            b_end = b_r[:, CHUNK - 1:CHUNK]
            m_prev = m_state[h][:, 0:1]
            c_prev, n_prev = c_state[h], n_state[h]

            d_log = jnp.where(causal, b_c - b_r + li_r, NEG_INF)
            a_c = b_end - b_c + li_c
            a_max = jnp.max(a_c, axis=0, keepdims=True)
            w_c = jnp.exp(a_c - a_max)
            inter = b_c + m_prev
            m_t = jnp.maximum(jnp.max(d_log, axis=1, keepdims=True), inter)
            d_w = jnp.exp(d_log - m_t)
            inter_w = jnp.exp(inter - m_t)

            q = q_all[h][rows, :]
            kf = k_all[h][rows, :]
            k = kf.astype(BF16)
            v = v_ref[rows, cols]
            s = _dot_nt(q, k) * d_w
            num = _dot(s.astype(BF16), v) + inter_w * _dot(q, c_prev.astype(BF16))
            qn = jnp.sum(q.astype(F32) * n_prev, axis=1, keepdims=True)
            den = jnp.sum(s, axis=1, keepdims=True) + inter_w * qn
            hh = num / jnp.maximum(jnp.abs(den), jnp.exp(-m_t))

            kw = kf * w_c
            kv = _dot(kw.T.astype(BF16), v)
            ks = jnp.sum(kw, axis=0, keepdims=True)
            m_new = jnp.maximum(b_end + m_prev, a_max)
            s_old = jnp.exp(b_end + m_prev - m_new)
            s_new = jnp.exp(a_max - m_new)
            c_state[h] = s_old * c_prev + s_new * kv
            n_state[h] = s_old * n_prev + s_new * ks
            m_state[h] = jnp.broadcast_to(m_new, (1, HEAD_DIM))

            out = jax.nn.sigmoid(og_ref[rows, cols].astype(F32)) * _group_norm(hh, gn_ref[:, cols])
            o_ref[rows, cols] = out.astype(o_ref.dtype)


def _mlstm(proj, gates, conv_w, conv_b, wq, wk, gn_g, *, batch, seq):
    ts = 512
    ns = seq // ts
    rowblk = lambda cb: pl.BlockSpec((ts, D_MLSTM), lambda b, s, cb=cb: (b * ns + s, cb))
    wspec = pl.BlockSpec((MLSTM_HEADS, HEAD_DIM, HEAD_DIM), lambda b, s: (0, 0, 0))
    return pl.pallas_call(
        functools.partial(_mlstm_kernel, ts=ts), grid=(batch, ns),
        in_specs=[rowblk(COL_MX // D_MLSTM), rowblk(COL_MV // D_MLSTM), rowblk(COL_MO // D_MLSTM),
                  pl.BlockSpec((ts, LANES), lambda b, s: (b * ns + s, 0)),
                  pl.BlockSpec((CONV_WIDTH, D_MLSTM), lambda b, s: (0, 0)),
                  pl.BlockSpec((1, D_MLSTM), lambda b, s: (0, 0)),
                  wspec, wspec,
                  pl.BlockSpec((1, D_MLSTM), lambda b, s: (0, 0))],
        out_specs=pl.BlockSpec((ts, D_MLSTM), lambda b, s: (b * ns + s, 0)),
        out_shape=jax.ShapeDtypeStruct((batch * seq, D_MLSTM), BF16),
        scratch_shapes=[pltpu.VMEM((ts + 8, D_MLSTM), F32),
                        pltpu.VMEM((MLSTM_HEADS, HEAD_DIM, HEAD_DIM), F32),
                        pltpu.VMEM((MLSTM_HEADS, 1, HEAD_DIM), F32),
                        pltpu.VMEM((MLSTM_HEADS, 1, HEAD_DIM), F32)],
        compiler_params=_cparams(2), name="mlstm",
    )(proj, proj, proj, gates, conv_w, conv_b.reshape(1, D_MLSTM), wq.astype(BF16),
      (wk * HEAD_DIM ** -0.5).astype(BF16), gn_g.reshape(1, D_MLSTM))


def _fcum_kernel(gates_ref, fcol_ref, frow_ref, carry_ref, *, ts):
    @pl.when(pl.program_id(1) == 0)
    def _():
        carry_ref[...] = jnp.zeros_like(carry_ref)

    tri = _lower_tri(CHUNK, BF16)
    carry = carry_ref[...]
    for c in range(ts // CHUNK):
        rows = slice(c * CHUNK, (c + 1) * CHUNK)
        cum = _cumsum_rows(tri, _log_sigmoid(gates_ref[rows, :])) + carry
        carry = cum[CHUNK - 1:CHUNK, :]
        fcol_ref[rows, :] = cum
        frow_ref[0, :, rows] = cum.T[LANE_FF:LANE_FF + FOX_HEADS, :]
    carry_ref[...] = carry


def _forget_cumsum(gates, *, batch, seq):
    ts = 512
    ns = seq // ts
    return pl.pallas_call(
        functools.partial(_fcum_kernel, ts=ts), grid=(batch, ns),
        in_specs=[pl.BlockSpec((ts, LANES), lambda b, s: (b * ns + s, 0))],
        out_specs=[pl.BlockSpec((ts, LANES), lambda b, s: (b * ns + s, 0)),
                   pl.BlockSpec((1, FOX_HEADS, ts), lambda b, s: (b, 0, s))],
        out_shape=[jax.ShapeDtypeStruct((batch * seq, LANES), F32),
                   jax.ShapeDtypeStruct((batch, FOX_HEADS, seq), F32)],
        scratch_shapes=[pltpu.VMEM((1, LANES), F32)],
        compiler_params=_cparams(2), name="forget_cumsum",
    )(gates)


def _fox_kernel(q_ref, k_ref, v_ref, fcol_ref, frow_ref, o_ref, s_ref, acc_ref, m_ref, l_ref, *, tq):
    h = pl.program_id(1)
    i = pl.program_id(2)
    q = q_ref[...]
    n_lt = tq // LANES
    lane = lax.broadcasted_iota(jnp.int32, (tq, LANES), 1)
    f_q = jnp.sum(jnp.where(lane == LANE_FF + h, fcol_ref[...], 0.0), axis=1, keepdims=True)
    f_q = jnp.broadcast_to(f_q, (tq, LANES))
    lane_tiles = lambda a: [a[:, c * LANES:(c + 1) * LANES] for c in range(n_lt)]

    def scores(j):
        start = pl.multiple_of(j * tq, tq)
        return start, _dot_nt(q, k_ref[pl.ds(start, tq), :]) - frow_ref[0, :, pl.ds(start, tq)]

    def keep(start, s):
        s_ref[:, pl.ds(start, tq)] = s
        m = m_ref[...]
        for s_c in lane_tiles(s):
            m = jnp.maximum(m, s_c)
        m_ref[...] = m

    def pass1(j, carry):
        keep(*scores(j))
        return carry

    m_ref[...] = jnp.full_like(m_ref, NEG_INF)
    lax.fori_loop(0, i, pass1, 0)
    start, s = scores(i)
    r_i = lax.broadcasted_iota(jnp.int32, (tq, tq), 0)
    c_i = lax.broadcasted_iota(jnp.int32, (tq, tq), 1)
    keep(start, jnp.where(c_i <= r_i, s, NEG_INF))

    m_row = jnp.max(m_ref[...], axis=1, keepdims=True) + f_q
    shift = f_q - m_row
    acc_ref[...] = jnp.zeros_like(acc_ref)
    l_ref[...] = jnp.zeros_like(l_ref)

    def pass2(j, carry):
        start = pl.multiple_of(j * tq, tq)
        ps = [jnp.exp(s_c + shift) for s_c in lane_tiles(s_ref[:, pl.ds(start, tq)])]
        l_ref[...] = l_ref[...] + sum(ps[1:], ps[0])
        p = jnp.concatenate(ps, axis=1).astype(BF16)
        acc_ref[...] = acc_ref[...] + _dot(p, v_ref[pl.ds(start, tq), :])
        return carry

    lax.fori_loop(0, i + 1, pass2, 0)
    l_row = jnp.sum(l_ref[...], axis=1, keepdims=True)
    o_ref[...] = (acc_ref[...] / l_row).astype(o_ref.dtype)


def _fox_attention(proj, fcol, frow, *, batch, seq):
    tq = 512
    nq = seq // tq
    cq, ck, cv = COL_FQ // HEAD_DIM, COL_FK // HEAD_DIM, COL_FV // HEAD_DIM
    return pl.pallas_call(
        functools.partial(_fox_kernel, tq=tq), grid=(batch, FOX_HEADS, nq),
        in_specs=[pl.BlockSpec((tq, HEAD_DIM), lambda b, h, i: (b * nq + i, cq + h)),
                  pl.BlockSpec((seq, HEAD_DIM), lambda b, h, i: (b, ck + h)),
                  pl.BlockSpec((seq, HEAD_DIM), lambda b, h, i: (b, cv + h)),
                  pl.BlockSpec((tq, LANES), lambda b, h, i: (b * nq + i, 0)),
                  pl.BlockSpec((1, 1, seq), lambda b, h, i: (b * FOX_HEADS + h, 0, 0))],
        out_specs=pl.BlockSpec((tq, HEAD_DIM), lambda b, h, i: (b * nq + i, h)),
        out_shape=jax.ShapeDtypeStruct((batch * seq, D_FOX), BF16),
        scratch_shapes=[pltpu.VMEM((tq, seq), F32), pltpu.VMEM((tq, HEAD_DIM), F32),
                        pltpu.VMEM((tq, LANES), F32), pltpu.VMEM((tq, LANES), F32)],
        compiler_params=_cparams(3), name="fox_attention",
    )(proj, proj, proj, fcol, frow.reshape(batch * FOX_HEADS, 1, seq))


def _route_kernel(logit_ref, eid_ref, gate_ref, rank_ref, cnt_ref, upper_ref, base_ref, *, tm):
    @pl.when(pl.program_id(0) == 0)
    def _():
        r = lax.broadcasted_iota(jnp.int32, (tm, tm), 0)
        c = lax.broadcasted_iota(jnp.int32, (tm, tm), 1)
        upper_ref[...] = jnp.where(r < c, 1.0, 0.0).astype(BF16)
        base_ref[...] = jnp.zeros_like(base_ref)

    lt = logit_ref[...].T
    e = lt[LANE_RE:LANE_RE + N_EXPERTS, :]
    g = lt[LANE_RG:LANE_RG + N_GROUPS, :]
    g_max = jnp.max(g, axis=0, keepdims=True)
    g_val = 1.0 / jnp.sum(jnp.exp(g - g_max), axis=0, keepdims=True)
    g_row = lax.broadcasted_iota(jnp.int32, (N_GROUPS, tm), 0).astype(F32)
    g_idx = jnp.min(jnp.where(g == g_max, g_row, float(N_GROUPS)), axis=0, keepdims=True)

    e_row_i = lax.broadcasted_iota(jnp.int32, (N_EXPERTS, tm), 0)
    e_row = e_row_i.astype(F32)
    e_grp = jnp.right_shift(e_row_i, EXPERTS_PER_GROUP.bit_length() - 1).astype(F32)
    el = jnp.where(e_grp == g_idx, e, NEG_INF)
    max1 = jnp.max(el, axis=0, keepdims=True)
    idx1 = jnp.min(jnp.where(el == max1, e_row, float(N_EXPERTS)), axis=0, keepdims=True)
    el2 = jnp.where(e_row == idx1, NEG_INF, el)
    max2 = jnp.max(el2, axis=0, keepdims=True)
    idx2 = jnp.min(jnp.where(el2 == max2, e_row, float(N_EXPERTS)), axis=0, keepdims=True)
    e_sum = jnp.sum(jnp.exp(el - max1), axis=0, keepdims=True)
    p1 = 1.0 / e_sum
    p2 = jnp.exp(max2 - max1) / e_sum
    p_sum = p1 + p2
    gate_ref[0:1, :] = g_val * (p1 / p_sum)
    gate_ref[1:2, :] = g_val * (p2 / p_sum)
    eid_ref[0:1, :] = idx1.astype(jnp.int32)
    eid_ref[1:2, :] = idx2.astype(jnp.int32)

    oh1 = e_row == idx1
    oh2 = e_row == idx2
    onehot = jnp.where(oh1 | oh2, 1.0, 0.0)
    before = _dot(onehot.astype(BF16), upper_ref[...]) + base_ref[:, 0:1]
    rank_ref[0:1, :] = jnp.sum(jnp.where(oh1, before, 0.0), axis=0, keepdims=True).astype(jnp.int32)
    rank_ref[1:2, :] = jnp.sum(jnp.where(oh2, before, 0.0), axis=0, keepdims=True).astype(jnp.int32)
    base_ref[...] = base_ref[...] + jnp.sum(onehot, axis=1, keepdims=True)
    cnt_ref[...] = base_ref[...]


def _route(logits):
    t = logits.shape[0]
    tm = min(1024, t)
    tok = lambda i: (0, i)
    return pl.pallas_call(
        functools.partial(_route_kernel, tm=tm), grid=(t // tm,),
        in_specs=[pl.BlockSpec((tm, LANES), lambda i: (i, 0))],
        out_specs=[pl.BlockSpec((2, tm), tok), pl.BlockSpec((2, tm), tok), pl.BlockSpec((2, tm), tok),
                   pl.BlockSpec((N_EXPERTS, LANES), lambda i: (0, 0))],
        out_shape=[jax.ShapeDtypeStruct((2, t), jnp.int32), jax.ShapeDtypeStruct((2, t), F32),
                   jax.ShapeDtypeStruct((2, t), jnp.int32),
                   jax.ShapeDtypeStruct((N_EXPERTS, LANES), F32)],
        scratch_shapes=[pltpu.VMEM((tm, tm), BF16), pltpu.VMEM((N_EXPERTS, LANES), F32)],
        compiler_params=_cparams(1), name="route",
    )(logits)


def _moe_kernel(tile_expert, n_active, row_id, h_hbm, w1_ref, w3_ref, w2_ref, o_hbm,
                xbuf, ybuf, gsem, ssem, *, tm, n_tokens, n_tiles):
    del tile_expert
    i = pl.program_id(0)
    n_act = n_active[0]
    last_step = jnp.minimum(n_act, n_tiles - 1)

    def token_of(rid):
        if n_tokens & (n_tokens - 1) == 0:
            return jnp.bitwise_and(rid, n_tokens - 1)
        return rid - n_tokens * ((rid >= n_tokens).astype(jnp.int32) + (rid >= 2 * n_tokens).astype(jnp.int32))

    def start_gather(tile, r, slot):
        tok = token_of(row_id[(tile + 1) * tm + r])
        pltpu.make_async_copy(h_hbm.at[pl.ds(tok, 1)], xbuf.at[slot, pl.ds(r, 1)], gsem.at[slot]).start()

    def start_scatter(tile, r):
        dst = row_id[(tile + 1) * tm + r]
        pltpu.make_async_copy(ybuf.at[pl.ds(r, 1)], o_hbm.at[pl.ds(dst, 1)], ssem).start()

    def wait_gather(slot):
        pltpu.make_async_copy(h_hbm.at[pl.ds(0, tm)], xbuf.at[slot], gsem.at[slot]).wait()

    def wait_scatter():
        pltpu.make_async_copy(ybuf, o_hbm.at[pl.ds(0, tm)], ssem).wait()

    def rows_loop(fn):
        def body(r, carry):
            fn(r)
            return carry
        lax.fori_loop(0, tm, body, 0, unroll=8)

    @pl.when(i <= last_step)
    def _():
        slot = lax.rem(i, 2)

        @pl.when(i == 0)
        def _():
            ybuf[...] = jnp.zeros_like(ybuf)
            rows_loop(lambda r: start_gather(0, r, 0))

        wait_gather(slot)
        x = xbuf[slot].astype(BF16)
        nxt = jnp.minimum(i + 1, n_tiles - 1)
        for r in range(tm):
            start_gather(nxt, r, 1 - slot)
            start_scatter(i - 1, r)

        h1 = _dot(x, w1_ref[0].astype(BF16))
        h3 = _dot(x, w3_ref[0].astype(BF16))
        act = (_silu(h1) * h3).astype(BF16)
        y = _dot(act, w2_ref[0].astype(BF16))

        wait_scatter()
        ybuf[...] = y

        @pl.when(i == last_step)
        def _():
            wait_gather(1 - slot)

            @pl.when(i < n_act)
            def _():
                rows_loop(lambda r: start_scatter(i, r))
                wait_scatter()


def _moe_experts(h2, w1, w3, w2, tile_expert, n_active, row_id, *, n_tiles):
    t, d = h2.shape
    de = w1.shape[2]
    tm = MOE_TILE
    wmap = lambda i, te, na, rid: (te[i], 0, 0)
    grid_spec = pltpu.PrefetchScalarGridSpec(
        num_scalar_prefetch=3, grid=(n_tiles,),
        in_specs=[pl.BlockSpec(memory_space=pl.ANY),
                  pl.BlockSpec((1, d, de), wmap), pl.BlockSpec((1, d, de), wmap),
                  pl.BlockSpec((1, de, d), wmap)],
        out_specs=pl.BlockSpec(memory_space=pl.ANY),
        scratch_shapes=[pltpu.VMEM((2, tm, d), F32), pltpu.VMEM((tm, d), F32),
                        pltpu.SemaphoreType.DMA((2,)), pltpu.SemaphoreType.DMA(())])
    return pl.pallas_call(
        functools.partial(_moe_kernel, tm=tm, n_tokens=t, n_tiles=n_tiles), grid_spec=grid_spec,
        out_shape=jax.ShapeDtypeStruct((2 * t + tm, d), F32),
        compiler_params=_cparams(1), name="moe_experts",
    )(tile_expert, n_active, row_id, h2, w1, w3, w2)


def _dispatch_plan(eid, rank, counts, *, n_tokens, n_tiles):
    tm = MOE_TILE
    cnt = counts[:, 0].astype(jnp.int32)
    tiles_e = (cnt + tm - 1) // tm
    tile_end = jnp.cumsum(tiles_e)
    tile_start = tile_end - tiles_e
    experts = jnp.arange(N_EXPERTS, dtype=jnp.int32)
    row_off = jnp.sum(jnp.where(eid[..., None] == experts, tile_start * tm, 0), axis=-1)
    dest = row_off + rank
    n_active = tile_end[-1]
    tile_ids = jnp.arange(n_tiles, dtype=jnp.int32)
    last_tile = jnp.minimum(tile_ids, n_active - 1)
    tile_expert = jnp.sum((tile_end[None, :] <= last_tile[:, None]).astype(jnp.int32), axis=-1)
    tile_expert = jnp.minimum(tile_expert, N_EXPERTS - 1)
    slot_row = (jnp.arange(n_tokens, dtype=jnp.int32)[None, :]
                + jnp.arange(2, dtype=jnp.int32)[:, None] * n_tokens)
    spare = 2 * n_tokens + jnp.arange((n_tiles + 1) * tm, dtype=jnp.int32) % tm
    row_id = spare.at[tm + dest.reshape(-1)].set(slot_row.reshape(-1))
    return tile_expert, n_active.reshape(1), row_id


W_IN_GATES0 = COL_MO + D_MLSTM
W_IN_FOX0 = W_IN_GATES0 + 2 * MLSTM_HEADS
W_IN_GATES1 = W_IN_FOX0 + 3 * D_FOX


def _wprep_kernel(w_ref, o_ref):
    s = HEAD_DIM ** -0.5
    w = w_ref[0]

    def put(dst, src):
        o_ref[0, :, dst:dst + src.shape[1]] = src.astype(BF16)

    put(COL_RQ, w[:, 0:COL_RK])
    put(COL_RK, w[:, COL_RK:COL_RV] * s)
    put(COL_RV, w[:, COL_RV:W_IN_GATES0])
    put(COL_FQ, w[:, W_IN_FOX0:W_IN_FOX0 + D_FOX] * s)
    put(COL_FK, w[:, W_IN_FOX0 + D_FOX:W_IN_GATES1])


def _rearranged_w_in(w_in):
    depth, d, n_in = w_in.shape
    tr = 256
    big = pl.pallas_call(
        _wprep_kernel, grid=(depth, d // tr),
        in_specs=[pl.BlockSpec((1, tr, n_in), lambda l, r: (l, r, 0))],
        out_specs=pl.BlockSpec((1, tr, N_PROJ), lambda l, r: (l, r, 0)),
        out_shape=jax.ShapeDtypeStruct((depth, d, N_PROJ), BF16),
        compiler_params=_cparams(2), name="w_in_prep",
    )(w_in)
    gate = jnp.concatenate([w_in[..., W_IN_GATES0:W_IN_FOX0],
                            w_in[..., W_IN_GATES1:W_IN_GATES1 + FOX_HEADS]], axis=-1)
    gate = jnp.pad(gate, ((0, 0), (0, 0), (0, LANES - gate.shape[-1])))
    return big, gate


def _lane_pad(v):
    return jnp.pad(v, ((0, 0), (0, LANES - v.shape[-1])))


def kernel(x, c, ada_w, ada_b, norm1_g, w_in, ret_gn_g, mlstm_conv_w, mlstm_conv_b, mlstm_wq, mlstm_wk,
           mlstm_i_b, mlstm_f_b, mlstm_gn_g, fox_f_b, w_out, norm2_g, router_group_w, router_group_b,
           router_expert_w, router_expert_b, moe_w1, moe_w3, moe_w2, final_g):
    batch, seq, d = x.shape
    depth = ada_w.shape[0]
    t = batch * seq
    n_tiles = (2 * t) // MOE_TILE + N_EXPERTS

    mod = _modulation(c, ada_w, ada_b)
    mod = mod.reshape(depth, batch, N_MOD, 1, d)
    w_big, w_gate = _rearranged_w_in(w_in)
    gate_bias = _lane_pad(jnp.concatenate([mlstm_i_b, mlstm_f_b, fox_f_b], axis=-1))
    w_route = jnp.concatenate([router_expert_w, router_group_w], axis=-1)
    w_route = jnp.pad(w_route, ((0, 0), (0, 0), (0, LANES - w_route.shape[-1])))
    route_bias = _lane_pad(jnp.concatenate([router_expert_b, router_group_b], axis=-1))
    w_out_b = w_out.astype(BF16)
    ret_consts = _retention_consts(seq)

    xf = x.reshape(t, d)
    moe_out, tok_gates, g2_prev = None, (), None
    for l in range(depth):
        sh1, sc1, g1, sh2, sc2, g2 = (mod[l, :, k] for k in range(N_MOD))
        if l == 0:
            h1, gates = _norm_call(xf, (), (), None, norm1_g[l], sc1, sh1, w_gate[l], gate_bias[l:l + 1],
                                   seq=seq, write_x=False, side="x1", h_dtype=BF16)
        else:
            xf, h1, gates = _norm_call(xf, (moe_out, moe_out), tok_gates, g2_prev, norm1_g[l], sc1, sh1,
                                       w_gate[l], gate_bias[l:l + 1], seq=seq, write_x=True, side="x1",
                                       h_dtype=BF16, add_row_offsets=(0, t))
        proj = _matmul(h1, w_big[l], BF16)
        y_ret = _retention(proj, ret_gn_g[l], ret_consts, batch=batch, seq=seq)
        y_m = _mlstm(proj, gates, mlstm_conv_w[l], mlstm_conv_b[l], mlstm_wq[l], mlstm_wk[l],
                     mlstm_gn_g[l], batch=batch, seq=seq)
        fcol, frow = _forget_cumsum(gates, batch=batch, seq=seq)
        y_f = _fox_attention(proj, fcol, frow, batch=batch, seq=seq)
        mix = _out_proj(y_ret, y_m, y_f, w_out_b[l], BF16)
        xf, h2, logits = _norm_call(xf, (mix,), (), g1, norm2_g[l], sc2, sh2, w_route[l],
                                    route_bias[l:l + 1], seq=seq, write_x=True, side="x3", h_dtype=F32)
        eid, gate, rank, counts = _route(logits)
        tile_expert, n_active, row_id = _dispatch_plan(eid, rank, counts, n_tokens=t, n_tiles=n_tiles)
        moe_out = _moe_experts(h2, moe_w1[l], moe_w3[l], moe_w2[l], tile_expert, n_active, row_id,
                               n_tiles=n_tiles)
        tok_gates = tuple(jnp.broadcast_to(gate[k][:, None], (t, LANES)) for k in range(2))
        g2_prev = g2
    zeros = jnp.zeros((batch, 1, d), F32)
    (out,) = _norm_call(xf, (moe_out, moe_out), tok_gates, g2_prev, final_g, zeros, zeros, None, None,
                        seq=seq, write_x=False, side=None, h_dtype=F32,
                        add_row_offsets=(0, t))
    return out.reshape(batch, seq, d)
```

```python
import functools

import jax
import jax.numpy as jnp
from jax import lax
from jax.experimental import pallas as pl
from jax.experimental.pallas import tpu as pltpu

F32 = jnp.float32
BF16 = jnp.bfloat16

HEAD_DIM = 128
LANES = 128
SUBLANES = 8
RET_HEADS = 4
MLSTM_HEADS = 4
FOX_HEADS = 8
D_RET = RET_HEADS * HEAD_DIM
D_MLSTM = MLSTM_HEADS * HEAD_DIM
D_FOX = FOX_HEADS * HEAD_DIM
CHUNK = 128
CONV_WIDTH = 4
ROPE_BASE = 10000.0
N_GROUPS = 4
EXPERTS_PER_GROUP = 8
N_EXPERTS = N_GROUPS * EXPERTS_PER_GROUP
N_MOD = 6
EPS = 1e-6
NEG_INF = float("-inf")

COL_RQ, COL_RK, COL_RV, COL_RG = 0, 512, 1024, 1536
COL_MX, COL_MV, COL_MO = 2048, 2560, 3072
COL_FQ, COL_FK, COL_FV = 3584, 4608, 5632
N_PROJ = 6656
LANE_MI, LANE_MF, LANE_FF = 0, 4, 8
LANE_RE, LANE_RG = 0, 32

VMEM_LIMIT = 56 * 1024 * 1024
MOE_TILE = 512


def _cparams(n_grid):
    return pltpu.CompilerParams(dimension_semantics=("arbitrary",) * n_grid,
                                vmem_limit_bytes=VMEM_LIMIT)


def _dot(a, b):
    return jnp.dot(a, b, preferred_element_type=F32)


def _dot_nt(a, b):
    return lax.dot_general(a, b, (((1,), (1,)), ((), ())), preferred_element_type=F32)


def _silu(x):
    return x * jax.nn.sigmoid(x)


def _log_sigmoid(x):
    return jnp.minimum(x, 0.0) - jnp.log(1.0 + jnp.exp(-jnp.abs(x)))


def _split3(x):
    hi = x.astype(BF16)
    r1 = x - hi.astype(F32)
    mid = r1.astype(BF16)
    lo = (r1 - mid.astype(F32)).astype(BF16)
    return hi, mid, lo


def _cumsum_rows(tri, x):
    hi, mid, lo = _split3(x)
    return _dot(tri, hi) + _dot(tri, mid) + _dot(tri, lo)


def _lower_tri(n, dtype):
    r = lax.broadcasted_iota(jnp.int32, (n, n), 0)
    c = lax.broadcasted_iota(jnp.int32, (n, n), 1)
    return jnp.where(c <= r, 1.0, 0.0).astype(dtype)


def _group_norm(y, gain):
    mu = jnp.mean(y, axis=-1, keepdims=True)
    yc = y - mu
    var = jnp.mean(yc * yc, axis=-1, keepdims=True)
    return yc * lax.rsqrt(var + EPS) * gain


def _mod_kernel(c_ref, w_ref, b_ref, o_ref, *, nb, tn):
    w = w_ref[0]
    rows = []
    for b in range(nb):
        cb = _silu(c_ref[b])
        parts = [jnp.sum(w[:, j * LANES:(j + 1) * LANES] * cb, axis=0, keepdims=True)
                 for j in range(tn // LANES)]
        rows.append(jnp.concatenate(parts, axis=1))
    o_ref[0] = jnp.concatenate(rows, axis=0) + b_ref[0]


def _modulation(c, ada_w, ada_b):
    depth, d, n = ada_w.shape
    nb = c.shape[0]
    tn = 512
    c_b = jnp.broadcast_to(c[:, :, None], (nb, d, LANES))
    return pl.pallas_call(
        functools.partial(_mod_kernel, nb=nb, tn=tn),
        grid=(depth, n // tn),
        in_specs=[pl.BlockSpec((nb, d, LANES), lambda l, j: (0, 0, 0)),
                  pl.BlockSpec((1, d, tn), lambda l, j: (l, 0, j)),
                  pl.BlockSpec((1, 1, tn), lambda l, j: (l, 0, j))],
        out_specs=pl.BlockSpec((1, nb, tn), lambda l, j: (l, 0, j)),
        out_shape=jax.ShapeDtypeStruct((depth, nb, n), F32),
        compiler_params=_cparams(2),
        name="adaln_mod",
    )(c_b, ada_w, ada_b.reshape(depth, 1, n))


def _from_token_major(ref, rows, nch):
    return jnp.concatenate([ref[pl.ds(c, rows, stride=nch), :] for c in range(nch)], axis=1)


def _to_token_major(ref, val, nch):
    rows = val.shape[0]
    for c in range(nch):
        ref[pl.ds(c, rows, stride=nch), :] = val[:, c * LANES:(c + 1) * LANES]


def _norm_kernel(*refs, n_add, tok_gate, write_x, side, h_dtype, add_tm, h_tm):
    it = iter(refs)
    x_ref = next(it)
    add_refs = [next(it) for _ in range(n_add)]
    tg_refs = [next(it) for _ in range(n_add)] if tok_gate else []
    gv_ref = next(it) if n_add else None
    g_ref, sc_ref, sh_ref = next(it), next(it), next(it)
    if side:
        ws_hi_ref, ws_lo_ref, bs_ref = next(it), next(it), next(it)
    xo_ref = next(it) if write_x else None
    h_ref = next(it)
    s_ref = next(it) if side else None

    x = x_ref[...]
    tm, d = x.shape
    nch = d // LANES
    if n_add:
        load = (lambda a: _from_token_major(a, tm, nch)) if add_tm else (lambda a: a[...].astype(F32))
        if tok_gate:
            upd = sum(load(a) * tg[:, 0:1] for a, tg in zip(add_refs, tg_refs))
        else:
            upd = sum(load(a) for a in add_refs)
        x = x + gv_ref[0] * upd
    if write_x:
        xo_ref[...] = x
    y = x * lax.rsqrt(jnp.mean(x * x, axis=-1, keepdims=True) + EPS)
    h = (y * g_ref[...]) * (1.0 + sc_ref[0]) + sh_ref[0]
    if h_tm:
        _to_token_major(h_ref, h.astype(h_dtype), nch)
    else:
        h_ref[...] = h.astype(h_dtype)
    if side:
        h_hi = h.astype(BF16)
        s = _dot(h_hi, ws_hi_ref[...])
        if side == "x3":
            h_lo = (h - h_hi.astype(F32)).astype(BF16)
            s = s + _dot(h_hi, ws_lo_ref[...]) + _dot(h_lo, ws_hi_ref[...])
        s_ref[...] = s + bs_ref[...]


def _norm_call(x, adds, tok_gates, gate_vec, g, sc, sh, side_w, side_b, *, seq, write_x, side,
               h_dtype, add_row_offsets=None, add_tm=False, h_tm=False):
    t, d = x.shape
    tm = 256
    nch = d // LANES
    nsb = seq // tm
    n_add = len(adds)
    tok_gate = bool(tok_gates)
    row = lambda i: (i, 0)
    vec = lambda i: (i // nsb, 0, 0)
    args, specs = [x], [pl.BlockSpec((tm, d), row)]
    for k, a in enumerate(adds):
        off = 0 if add_row_offsets is None else add_row_offsets[k] // tm
        args.append(a)
        blk = (tm * nch, LANES) if add_tm else (tm, d)
        specs.append(pl.BlockSpec(blk, lambda i, off=off: (i + off, 0)))
    for tg in tok_gates:
        args.append(tg)
        specs.append(pl.BlockSpec((tm, LANES), row))
    if n_add:
        args.append(gate_vec)
        specs.append(pl.BlockSpec((1, 1, d), vec))
    args += [g.reshape(1, d), sc, sh]
    specs += [pl.BlockSpec((1, d), lambda i: (0, 0)), pl.BlockSpec((1, 1, d), vec),
              pl.BlockSpec((1, 1, d), vec)]
    if side:
        w_hi = side_w.astype(BF16)
        w_lo = (side_w - w_hi.astype(F32)).astype(BF16)
        args += [w_hi, w_lo, side_b]
        specs += [pl.BlockSpec((d, LANES), lambda i: (0, 0)), pl.BlockSpec((d, LANES), lambda i: (0, 0)),
                  pl.BlockSpec((1, LANES), lambda i: (0, 0))]
    out_shape, out_specs = [], []
    if write_x:
        out_shape.append(jax.ShapeDtypeStruct((t, d), F32))
        out_specs.append(pl.BlockSpec((tm, d), row))
    if h_tm:
        out_shape.append(jax.ShapeDtypeStruct((t * nch, LANES), h_dtype))
        out_specs.append(pl.BlockSpec((tm * nch, LANES), row))
    else:
        out_shape.append(jax.ShapeDtypeStruct((t, d), h_dtype))
        out_specs.append(pl.BlockSpec((tm, d), row))
    if side:
        out_shape.append(jax.ShapeDtypeStruct((t, LANES), F32))
        out_specs.append(pl.BlockSpec((tm, LANES), row))
    return pl.pallas_call(
        functools.partial(_norm_kernel, n_add=n_add, tok_gate=tok_gate, write_x=write_x, side=side,
                          h_dtype=h_dtype, add_tm=add_tm, h_tm=h_tm),
        grid=(t // tm,), in_specs=specs, out_specs=out_specs, out_shape=out_shape,
        compiler_params=_cparams(1), name="resid_norm",
    )(*args)


def _mm_kernel(a_ref, w_ref, o_ref):
    o_ref[...] = _dot(a_ref[...], w_ref[0]).astype(o_ref.dtype)


def _matmul(a, w_stack, layer, out_dtype, tm=1024, tn=512):
    m, k = a.shape
    n = w_stack.shape[2]
    tm = min(tm, m)
    return pl.pallas_call(
        _mm_kernel, grid=(m // tm, n // tn),
        in_specs=[pl.BlockSpec((tm, k), lambda i, j: (i, 0)),
                  pl.BlockSpec((1, k, tn), lambda i, j: (layer, 0, j))],
        out_specs=pl.BlockSpec((tm, tn), lambda i, j: (i, j)),
        out_shape=jax.ShapeDtypeStruct((m, n), out_dtype),
        compiler_params=_cparams(2), name="in_proj",
    )(a, w_stack)


def _mm3_kernel(a1_ref, a2_ref, a3_ref, w_ref, o_ref):
    k1, k2 = a1_ref.shape[1], a2_ref.shape[1]
    acc = _dot(a1_ref[...], w_ref[0, 0:k1, :])
    acc = acc + _dot(a2_ref[...], w_ref[0, k1:k1 + k2, :])
    acc = acc + _dot(a3_ref[...], w_ref[0, k1 + k2:, :])
    o_ref[...] = acc.astype(o_ref.dtype)


def _out_proj(y_ret, y_m, y_f, w_stack, layer, out_dtype, tm=1024, tn=512):
    m = y_ret.shape[0]
    _, k, n = w_stack.shape
    tm = min(tm, m)
    return pl.pallas_call(
        _mm3_kernel, grid=(m // tm, n // tn),
        in_specs=[pl.BlockSpec((tm, y_ret.shape[1]), lambda i, j: (i, 0)),
                  pl.BlockSpec((tm, y_m.shape[1]), lambda i, j: (i, 0)),
                  pl.BlockSpec((tm, y_f.shape[1]), lambda i, j: (i, 0)),
                  pl.BlockSpec((1, k, tn), lambda i, j: (layer, 0, j))],
        out_specs=pl.BlockSpec((tm, tn), lambda i, j: (i, j)),
        out_shape=jax.ShapeDtypeStruct((m, n), out_dtype),
        compiler_params=_cparams(2), name="out_proj",
    )(y_ret, y_m, y_f, w_stack)


def _ret_kernel(q_ref, k_ref, v_ref, g_ref, cos_ref, sin_ref, decay_ref, zeta_ref, xi_ref, gam_ref,
                gn_ref, o_ref, state_ref, *, ts):
    @pl.when(pl.program_id(1) == 0)
    def _():
        state_ref[...] = jnp.zeros_like(state_ref)

    for c in range(ts // CHUNK):
        rows = slice(c * CHUNK, (c + 1) * CHUNK)
        cosv, sinv = cos_ref[rows, :], sin_ref[rows, :]
        for h in range(RET_HEADS):
            cols = slice(h * HEAD_DIM, (h + 1) * HEAD_DIM)
            q = q_ref[rows, cols].astype(F32)
            k = k_ref[rows, cols].astype(F32)
            qr = q * cosv + pltpu.roll(q, HEAD_DIM // 2, 1) * sinv
            kr = k * cosv + pltpu.roll(k, HEAD_DIM // 2, 1) * sinv
            qb, kb, vb = qr.astype(BF16), kr.astype(BF16), v_ref[rows, cols]
            s = _dot_nt(qb, kb) * decay_ref[h]
            state = state_ref[h]
            y = _dot(s.astype(BF16), vb) + _dot(qb, state.astype(BF16)) * xi_ref[h]
            kz_t = (kr * zeta_ref[h]).T.astype(BF16)
            state_ref[h] = state * gam_ref[h] + _dot(kz_t, vb)
            out = _group_norm(y, gn_ref[:, cols]) * _silu(g_ref[rows, cols].astype(F32))
            o_ref[rows, cols] = out.astype(o_ref.dtype)


def _retention(proj, gn_g, consts, *, batch, seq):
    ts = 512
    ns = seq // ts
    cosf, sins, decay, zeta, xi, gam = consts
    rowblk = lambda cb: pl.BlockSpec((ts, D_RET), lambda b, s, cb=cb: (b * ns + s, cb))
    full3 = pl.BlockSpec((RET_HEADS, CHUNK, HEAD_DIM), lambda b, s: (0, 0, 0))
    return pl.pallas_call(
        functools.partial(_ret_kernel, ts=ts), grid=(batch, ns),
        in_specs=[rowblk(COL_RQ // D_RET), rowblk(COL_RK // D_RET), rowblk(COL_RV // D_RET),
                  rowblk(COL_RG // D_RET),
                  pl.BlockSpec((ts, HEAD_DIM), lambda b, s: (s, 0)),
                  pl.BlockSpec((ts, HEAD_DIM), lambda b, s: (s, 0)),
                  full3, full3, full3,
                  pl.BlockSpec((RET_HEADS, 1, HEAD_DIM), lambda b, s: (0, 0, 0)),
                  pl.BlockSpec((1, D_RET), lambda b, s: (0, 0))],
        out_specs=pl.BlockSpec((ts, D_RET), lambda b, s: (b * ns + s, 0)),
        out_shape=jax.ShapeDtypeStruct((batch * seq, D_RET), BF16),
        scratch_shapes=[pltpu.VMEM((RET_HEADS, HEAD_DIM, HEAD_DIM), F32)],
        compiler_params=_cparams(2), name="retention",
    )(proj, proj, proj, proj, cosf, sins, decay, zeta, xi, gam, gn_g.reshape(1, D_RET))


def _retention_consts(seq):
    h = RET_HEADS
    log_gamma = jnp.log1p(-jnp.power(2.0, -5.0 - jnp.arange(h, dtype=F32)))
    pos = jnp.arange(CHUNK, dtype=F32)
    diff = pos[:, None] - pos[None, :]
    decay = jnp.where(diff >= 0, jnp.exp(jnp.maximum(diff, 0.0)[None] * log_gamma[:, None, None]), 0.0)
    zeta = jnp.exp((CHUNK - 1 - pos)[None, :] * log_gamma[:, None])
    xi = jnp.exp((pos + 1)[None, :] * log_gamma[:, None])
    gam = jnp.exp(CHUNK * log_gamma)
    bcast = lambda v: jnp.broadcast_to(v[:, :, None], (h, CHUNK, HEAD_DIM))
    half = HEAD_DIM // 2
    inv = ROPE_BASE ** (-jnp.arange(half, dtype=F32) / half)
    ang = jnp.arange(seq, dtype=F32)[:, None] * inv[None, :]
    cosf = jnp.concatenate([jnp.cos(ang), jnp.cos(ang)], axis=1)
    sins = jnp.concatenate([-jnp.sin(ang), jnp.sin(ang)], axis=1)
    gam_b = jnp.broadcast_to(gam[:, None, None], (h, 1, HEAD_DIM))
    return cosf, sins, decay, bcast(zeta), bcast(xi), gam_b


def _mlstm_kernel(x_ref, v_ref, og_ref, gates_ref, cw_ref, cb_ref, wq_ref, wk_ref, gn_ref, o_ref,
                  xbuf, c_state, n_state, m_state, *, ts):
    @pl.when(pl.program_id(1) == 0)
    def _():
        c_state[...] = jnp.zeros_like(c_state)
        n_state[...] = jnp.zeros_like(n_state)
        m_state[...] = jnp.full_like(m_state, NEG_INF)
        xbuf[0:8, :] = jnp.zeros((8, D_MLSTM), F32)

    x = x_ref[...].astype(F32)
    xbuf[8:ts + 8, :] = x
    conv = cb_ref[...]
    for j in range(CONV_WIDTH):
        start = 8 - (CONV_WIDTH - 1) + j
        conv = conv + cw_ref[j:j + 1, :] * xbuf[start:start + ts, :]
    xbuf[0:8, :] = x[ts - 8:ts, :]
    xc = _silu(conv).astype(BF16)

    tri = _lower_tri(CHUNK, BF16)
    r_i = lax.broadcasted_iota(jnp.int32, (CHUNK, CHUNK), 0)
    c_i = lax.broadcasted_iota(jnp.int32, (CHUNK, CHUNK), 1)
    causal = c_i <= r_i

    q_all = [_dot(xc[:, h * HEAD_DIM:(h + 1) * HEAD_DIM], wq_ref[h]).astype(BF16) for h in range(MLSTM_HEADS)]
    k_all = [_dot(xc[:, h * HEAD_DIM:(h + 1) * HEAD_DIM], wk_ref[h]) for h in range(MLSTM_HEADS)]

    for c in range(ts // CHUNK):
        rows = slice(c * CHUNK, (c + 1) * CHUNK)
        gc = gates_ref[rows, :]
        gt = gc.T
        b_col = _cumsum_rows(tri, _log_sigmoid(gc))
        b_row = b_col.T
        for h in range(MLSTM_HEADS):
            cols = slice(h * HEAD_DIM, (h + 1) * HEAD_DIM)
            li_r = gt[LANE_MI + h:LANE_MI + h + 1, :]
            li_c = gc[:, LANE_MI + h:LANE_MI + h + 1]
            b_c = b_col[:, LANE_MF + h:LANE_MF + h + 1]
            b_r = b_row[LANE_MF + h:LANE_MF + h + 1, :]
            b_end = b_r[:, CHUNK - 1:CHUNK]
            m_prev = m_state[h][:, 0:1]
            c_prev, n_prev = c_state[h], n_state[h]

            d_log = jnp.where(causal, b_c - b_r + li_r, NEG_INF)
            a_c = b_end - b_c + li_c
            a_max = jnp.max(a_c, axis=0, keepdims=True)
            w_c = jnp.exp(a_c - a_max)
            inter = b_c + m_prev
            m_t = jnp.maximum(jnp.max(d_log, axis=1, keepdims=True), inter)
            d_w = jnp.exp(d_log - m_t)
            inter_w = jnp.exp(inter - m_t)

            q = q_all[h][rows, :]
            kf = k_all[h][rows, :]
            k = kf.astype(BF16)
            v = v_ref[rows, cols]
            s = _dot_nt(q, k) * d_w
            num = _dot(s.astype(BF16), v) + inter_w * _dot(q, c_prev.astype(BF16))
            qn = jnp.sum(q.astype(F32) * n_prev, axis=1, keepdims=True)
            den = jnp.sum(s, axis=1, keepdims=True) + inter_w * qn
            hh = num / jnp.maximum(jnp.abs(den), jnp.exp(-m_t))

            kw = kf * w_c
            kv = _dot(kw.T.astype(BF16), v)
            ks = jnp.sum(kw, axis=0, keepdims=True)
            m_new = jnp.maximum(b_end + m_prev, a_max)
            s_old = jnp.exp(b_end + m_prev - m_new)
            s_new = jnp.exp(a_max - m_new)
            c_state[h] = s_old * c_prev + s_new * kv
            n_state[h] = s_old * n_prev + s_new * ks
            m_state[h] = jnp.broadcast_to(m_new, (1, HEAD_DIM))

            out = jax.nn.sigmoid(og_ref[rows, cols].astype(F32)) * _group_norm(hh, gn_ref[:, cols])
            o_ref[rows, cols] = out.astype(o_ref.dtype)


def _mlstm(proj, gates, conv_w, conv_b, wq, wk, gn_g, *, batch, seq):
    ts = 512
    ns = seq // ts
    rowblk = lambda cb: pl.BlockSpec((ts, D_MLSTM), lambda b, s, cb=cb: (b * ns + s, cb))
    wspec = pl.BlockSpec((MLSTM_HEADS, HEAD_DIM, HEAD_DIM), lambda b, s: (0, 0, 0))
    return pl.pallas_call(
        functools.partial(_mlstm_kernel, ts=ts), grid=(batch, ns),
        in_specs=[rowblk(COL_MX // D_MLSTM), rowblk(COL_MV // D_MLSTM), rowblk(COL_MO // D_MLSTM),
                  pl.BlockSpec((ts, LANES), lambda b, s: (b * ns + s, 0)),
                  pl.BlockSpec((CONV_WIDTH, D_MLSTM), lambda b, s: (0, 0)),
                  pl.BlockSpec((1, D_MLSTM), lambda b, s: (0, 0)),
                  wspec, wspec,
                  pl.BlockSpec((1, D_MLSTM), lambda b, s: (0, 0))],
        out_specs=pl.BlockSpec((ts, D_MLSTM), lambda b, s: (b * ns + s, 0)),
        out_shape=jax.ShapeDtypeStruct((batch * seq, D_MLSTM), BF16),
        scratch_shapes=[pltpu.VMEM((ts + 8, D_MLSTM), F32),
                        pltpu.VMEM((MLSTM_HEADS, HEAD_DIM, HEAD_DIM), F32),
                        pltpu.VMEM((MLSTM_HEADS, 1, HEAD_DIM), F32),
                        pltpu.VMEM((MLSTM_HEADS, 1, HEAD_DIM), F32)],
        compiler_params=_cparams(2), name="mlstm",
    )(proj, proj, proj, gates, conv_w, conv_b.reshape(1, D_MLSTM), wq.astype(BF16),
      (wk * HEAD_DIM ** -0.5).astype(BF16), gn_g.reshape(1, D_MLSTM))


def _fcum_kernel(gates_ref, fcol_ref, frow_ref, carry_ref, *, ts):
    @pl.when(pl.program_id(1) == 0)
    def _():
        carry_ref[...] = jnp.zeros_like(carry_ref)

    tri = _lower_tri(CHUNK, BF16)
    carry = carry_ref[...]
    for c in range(ts // CHUNK):
        rows = slice(c * CHUNK, (c + 1) * CHUNK)
        cum = _cumsum_rows(tri, _log_sigmoid(gates_ref[rows, :])) + carry
        carry = cum[CHUNK - 1:CHUNK, :]
        fcol_ref[rows, :] = cum
        frow_ref[0, :, rows] = cum.T[LANE_FF:LANE_FF + FOX_HEADS, :]
    carry_ref[...] = carry


def _forget_cumsum(gates, *, batch, seq):
    ts = 512
    ns = seq // ts
    return pl.pallas_call(
        functools.partial(_fcum_kernel, ts=ts), grid=(batch, ns),
        in_specs=[pl.BlockSpec((ts, LANES), lambda b, s: (b * ns + s, 0))],
        out_specs=[pl.BlockSpec((ts, LANES), lambda b, s: (b * ns + s, 0)),
                   pl.BlockSpec((1, FOX_HEADS, ts), lambda b, s: (b, 0, s))],
        out_shape=[jax.ShapeDtypeStruct((batch * seq, LANES), F32),
                   jax.ShapeDtypeStruct((batch, FOX_HEADS, seq), F32)],
        scratch_shapes=[pltpu.VMEM((1, LANES), F32)],
        compiler_params=_cparams(2), name="forget_cumsum",
    )(gates)


def _fox_kernel(q_ref, k_ref, v_ref, fcol_ref, frow_ref, o_ref, s_ref, acc_ref, m_ref, l_ref, *, tq):
    h = pl.program_id(1)
    i = pl.program_id(2)
    q = q_ref[...]
    n_lt = tq // LANES
    lane = lax.broadcasted_iota(jnp.int32, (tq, LANES), 1)
    f_q = jnp.sum(jnp.where(lane == LANE_FF + h, fcol_ref[...], 0.0), axis=1, keepdims=True)
    f_q = jnp.broadcast_to(f_q, (tq, LANES))
    lane_tiles = lambda a: [a[:, c * LANES:(c + 1) * LANES] for c in range(n_lt)]

    def scores(j):
        start = pl.multiple_of(j * tq, tq)
        return start, _dot_nt(q, k_ref[pl.ds(start, tq), :]) - frow_ref[0, :, pl.ds(start, tq)]

    def keep(start, s):
        s_ref[:, pl.ds(start, tq)] = s
        m = m_ref[...]
        for s_c in lane_tiles(s):
            m = jnp.maximum(m, s_c)
        m_ref[...] = m

    def pass1(j, carry):
        keep(*scores(j))
        return carry

    m_ref[...] = jnp.full_like(m_ref, NEG_INF)
    lax.fori_loop(0, i, pass1, 0)
    start, s = scores(i)
    r_i = lax.broadcasted_iota(jnp.int32, (tq, tq), 0)
    c_i = lax.broadcasted_iota(jnp.int32, (tq, tq), 1)
    keep(start, jnp.where(c_i <= r_i, s, NEG_INF))

    m_row = jnp.max(m_ref[...], axis=1, keepdims=True) + f_q
    shift = f_q - m_row
    acc_ref[...] = jnp.zeros_like(acc_ref)
    l_ref[...] = jnp.zeros_like(l_ref)

    def pass2(j, carry):
        start = pl.multiple_of(j * tq, tq)
        ps = [jnp.exp(s_c + shift) for s_c in lane_tiles(s_ref[:, pl.ds(start, tq)])]
        l_ref[...] = l_ref[...] + sum(ps[1:], ps[0])
        p = jnp.concatenate(ps, axis=1).astype(BF16)
        acc_ref[...] = acc_ref[...] + _dot(p, v_ref[pl.ds(start, tq), :])
        return carry

    lax.fori_loop(0, i + 1, pass2, 0)
    l_row = jnp.sum(l_ref[...], axis=1, keepdims=True)
    o_ref[...] = (acc_ref[...] / l_row).astype(o_ref.dtype)


def _fox_attention(proj, fcol, frow, *, batch, seq):
    tq = 512
    nq = seq // tq
    cq, ck, cv = COL_FQ // HEAD_DIM, COL_FK // HEAD_DIM, COL_FV // HEAD_DIM
    return pl.pallas_call(
        functools.partial(_fox_kernel, tq=tq), grid=(batch, FOX_HEADS, nq),
        in_specs=[pl.BlockSpec((tq, HEAD_DIM), lambda b, h, i: (b * nq + i, cq + h)),
                  pl.BlockSpec((seq, HEAD_DIM), lambda b, h, i: (b, ck + h)),
                  pl.BlockSpec((seq, HEAD_DIM), lambda b, h, i: (b, cv + h)),
                  pl.BlockSpec((tq, LANES), lambda b, h, i: (b * nq + i, 0)),
                  pl.BlockSpec((1, 1, seq), lambda b, h, i: (b * FOX_HEADS + h, 0, 0))],
        out_specs=pl.BlockSpec((tq, HEAD_DIM), lambda b, h, i: (b * nq + i, h)),
        out_shape=jax.ShapeDtypeStruct((batch * seq, D_FOX), BF16),
        scratch_shapes=[pltpu.VMEM((tq, seq), F32), pltpu.VMEM((tq, HEAD_DIM), F32),
                        pltpu.VMEM((tq, LANES), F32), pltpu.VMEM((tq, LANES), F32)],
        compiler_params=_cparams(3), name="fox_attention",
    )(proj, proj, proj, fcol, frow.reshape(batch * FOX_HEADS, 1, seq))


def _route_kernel(logit_ref, eid_ref, gate_ref, rank_ref, cnt_ref, upper_ref, base_ref, *, tm):
    @pl.when(pl.program_id(0) == 0)
    def _():
        r = lax.broadcasted_iota(jnp.int32, (tm, tm), 0)
        c = lax.broadcasted_iota(jnp.int32, (tm, tm), 1)
        upper_ref[...] = jnp.where(r < c, 1.0, 0.0).astype(BF16)
        base_ref[...] = jnp.zeros_like(base_ref)

    lt = logit_ref[...].T
    e = lt[LANE_RE:LANE_RE + N_EXPERTS, :]
    g = lt[LANE_RG:LANE_RG + N_GROUPS, :]
    g_max = jnp.max(g, axis=0, keepdims=True)
    g_val = 1.0 / jnp.sum(jnp.exp(g - g_max), axis=0, keepdims=True)
    g_row = lax.broadcasted_iota(jnp.int32, (N_GROUPS, tm), 0).astype(F32)
    g_idx = jnp.min(jnp.where(g == g_max, g_row, float(N_GROUPS)), axis=0, keepdims=True)

    e_row_i = lax.broadcasted_iota(jnp.int32, (N_EXPERTS, tm), 0)
    e_row = e_row_i.astype(F32)
    e_grp = jnp.right_shift(e_row_i, EXPERTS_PER_GROUP.bit_length() - 1).astype(F32)
    el = jnp.where(e_grp == g_idx, e, NEG_INF)
    max1 = jnp.max(el, axis=0, keepdims=True)
    idx1 = jnp.min(jnp.where(el == max1, e_row, float(N_EXPERTS)), axis=0, keepdims=True)
    el2 = jnp.where(e_row == idx1, NEG_INF, el)
    max2 = jnp.max(el2, axis=0, keepdims=True)
    idx2 = jnp.min(jnp.where(el2 == max2, e_row, float(N_EXPERTS)), axis=0, keepdims=True)
    e_sum = jnp.sum(jnp.exp(el - max1), axis=0, keepdims=True)
    p1 = 1.0 / e_sum
    p2 = jnp.exp(max2 - max1) / e_sum
    p_sum = p1 + p2
    gate_ref[0:1, :] = g_val * (p1 / p_sum)
    gate_ref[1:2, :] = g_val * (p2 / p_sum)
    eid_ref[0:1, :] = idx1.astype(jnp.int32)
    eid_ref[1:2, :] = idx2.astype(jnp.int32)

    oh1 = e_row == idx1
    oh2 = e_row == idx2
    onehot = jnp.where(oh1 | oh2, 1.0, 0.0)
    before = _dot(onehot.astype(BF16), upper_ref[...]) + base_ref[:, 0:1]
    rank_ref[0:1, :] = jnp.sum(jnp.where(oh1, before, 0.0), axis=0, keepdims=True).astype(jnp.int32)
    rank_ref[1:2, :] = jnp.sum(jnp.where(oh2, before, 0.0), axis=0, keepdims=True).astype(jnp.int32)
    base_ref[...] = base_ref[...] + jnp.sum(onehot, axis=1, keepdims=True)
    cnt_ref[...] = base_ref[...]


def _route(logits):
    t = logits.shape[0]
    tm = min(1024, t)
    tok = lambda i: (0, i)
    return pl.pallas_call(
        functools.partial(_route_kernel, tm=tm), grid=(t // tm,),
        in_specs=[pl.BlockSpec((tm, LANES), lambda i: (i, 0))],
        out_specs=[pl.BlockSpec((2, tm), tok), pl.BlockSpec((2, tm), tok), pl.BlockSpec((2, tm), tok),
                   pl.BlockSpec((N_EXPERTS, LANES), lambda i: (0, 0))],
        out_shape=[jax.ShapeDtypeStruct((2, t), jnp.int32), jax.ShapeDtypeStruct((2, t), F32),
                   jax.ShapeDtypeStruct((2, t), jnp.int32),
                   jax.ShapeDtypeStruct((N_EXPERTS, LANES), F32)],
        scratch_shapes=[pltpu.VMEM((tm, tm), BF16), pltpu.VMEM((N_EXPERTS, LANES), F32)],
        compiler_params=_cparams(1), name="route",
    )(logits)


def _moe_kernel(tile_expert, n_active, row_id, h_hbm, w1_ref, w3_ref, w2_ref, o_hbm,
                xbuf, xb, ybuf, gsem, ssem, *, tm, nch, n_tokens, n_tiles):
    del tile_expert
    i = pl.program_id(0)
    n_act = n_active[0]
    last_step = jnp.minimum(n_act, n_tiles - 1)

    def token_of(rid):
        if n_tokens & (n_tokens - 1) == 0:
            return jnp.bitwise_and(rid, n_tokens - 1)
        return rid - n_tokens * ((rid >= n_tokens).astype(jnp.int32) + (rid >= 2 * n_tokens).astype(jnp.int32))

    def hbm_token(ref, tok):
        return ref.at[pl.ds(pl.multiple_of(tok * nch, nch), nch)]

    def vmem_row(ref, r):
        if isinstance(r, int):
            return ref.at[r // SUBLANES, :, r % SUBLANES, :]
        return ref.at[r // SUBLANES, :, lax.rem(r, SUBLANES), :]

    def start_gather(tile, r):
        tok = token_of(row_id[(tile + 1) * tm + r])
        pltpu.make_async_copy(hbm_token(h_hbm, tok), vmem_row(xbuf, r), gsem).start()

    def start_scatter(tile, r):
        dst = row_id[(tile + 1) * tm + r]
        pltpu.make_async_copy(vmem_row(ybuf, r), hbm_token(o_hbm, dst), ssem).start()

    def wait_rows(sem):
        for _ in range(tm):
            pltpu.make_async_copy(hbm_token(h_hbm, 0), vmem_row(xbuf, 0), sem).wait()

    def rows_loop(fn):
        def body(r, carry):
            fn(r)
            return carry
        lax.fori_loop(0, tm, body, 0, unroll=8)

    @pl.when(i <= last_step)
    def _():
        @pl.when(i == 0)
        def _():
            ybuf[...] = jnp.zeros_like(ybuf)
            rows_loop(lambda r: start_gather(0, r))

        wait_rows(gsem)
        xb[...] = jnp.concatenate([xbuf[:, c].reshape(tm, LANES) for c in range(nch)], axis=1).astype(BF16)
        nxt = jnp.minimum(i + 1, n_tiles - 1)
        for r in range(tm):
            start_gather(nxt, r)
            start_scatter(i - 1, r)

        x = xb[...]
        h1 = _dot(x, w1_ref[0, 0].astype(BF16))
        h3 = _dot(x, w3_ref[0, 0].astype(BF16))
        act = (_silu(h1) * h3).astype(BF16)
        y = _dot(act, w2_ref[0, 0].astype(BF16))

        wait_rows(ssem)
        for c in range(nch):
            ybuf[:, c] = y[:, c * LANES:(c + 1) * LANES].reshape(tm // SUBLANES, SUBLANES, LANES)

        @pl.when(i == last_step)
        def _():
            wait_rows(gsem)

            @pl.when(i < n_act)
            def _():
                rows_loop(lambda r: start_scatter(i, r))
                wait_rows(ssem)


def _moe_experts(h2_tm, w1, w3, w2, layer, tile_expert, n_active, row_id, *, n_tiles):
    _, _, d, de = w1.shape
    nch = d // LANES
    t = h2_tm.shape[0] // nch
    tm = MOE_TILE
    wmap = lambda i, te, na, rid: (layer, te[i], 0, 0)
    grid_spec = pltpu.PrefetchScalarGridSpec(
        num_scalar_prefetch=3, grid=(n_tiles,),
        in_specs=[pl.BlockSpec(memory_space=pl.ANY),
                  pl.BlockSpec((1, 1, d, de), wmap), pl.BlockSpec((1, 1, d, de), wmap),
                  pl.BlockSpec((1, 1, de, d), wmap)],
        out_specs=pl.BlockSpec(memory_space=pl.ANY),
        scratch_shapes=[pltpu.VMEM((tm // SUBLANES, nch, SUBLANES, LANES), F32), pltpu.VMEM((tm, d), BF16),
                        pltpu.VMEM((tm // SUBLANES, nch, SUBLANES, LANES), F32),
                        pltpu.SemaphoreType.DMA(()), pltpu.SemaphoreType.DMA(())])
    return pl.pallas_call(
        functools.partial(_moe_kernel, tm=tm, nch=nch, n_tokens=t, n_tiles=n_tiles), grid_spec=grid_spec,
        out_shape=jax.ShapeDtypeStruct(((2 * t + tm) * nch, LANES), F32),
        compiler_params=_cparams(1), name="moe_experts",
    )(tile_expert, n_active, row_id, h2_tm, w1, w3, w2)


def _dispatch_plan(eid, rank, counts, *, n_tokens, n_tiles):
    tm = MOE_TILE
    cnt = counts[:, 0].astype(jnp.int32)
    tiles_e = (cnt + tm - 1) // tm
    tile_end = jnp.cumsum(tiles_e)
    tile_start = tile_end - tiles_e
    experts = jnp.arange(N_EXPERTS, dtype=jnp.int32)
    row_off = jnp.sum(jnp.where(eid[..., None] == experts, tile_start * tm, 0), axis=-1)
    dest = row_off + rank
    n_active = tile_end[-1]
    tile_ids = jnp.arange(n_tiles, dtype=jnp.int32)
    last_tile = jnp.minimum(tile_ids, n_active - 1)
    tile_expert = jnp.sum((tile_end[None, :] <= last_tile[:, None]).astype(jnp.int32), axis=-1)
    tile_expert = jnp.minimum(tile_expert, N_EXPERTS - 1)
    slot_row = (jnp.arange(n_tokens, dtype=jnp.int32)[None, :]
                + jnp.arange(2, dtype=jnp.int32)[:, None] * n_tokens)
    spare = 2 * n_tokens + jnp.arange((n_tiles + 1) * tm, dtype=jnp.int32) % tm
    row_id = spare.at[tm + dest.reshape(-1)].set(slot_row.reshape(-1))
    return tile_expert, n_active.reshape(1), row_id


W_IN_GATES0 = COL_MO + D_MLSTM
W_IN_FOX0 = W_IN_GATES0 + 2 * MLSTM_HEADS
W_IN_GATES1 = W_IN_FOX0 + 3 * D_FOX


def _wprep_kernel(w_ref, o_ref):
    s = HEAD_DIM ** -0.5
    w = w_ref[0]

    def put(dst, src):
        o_ref[0, :, dst:dst + src.shape[1]] = src.astype(BF16)

    put(COL_RQ, w[:, 0:COL_RK])
    put(COL_RK, w[:, COL_RK:COL_RV] * s)
    put(COL_RV, w[:, COL_RV:W_IN_GATES0])
    put(COL_FQ, w[:, W_IN_FOX0:W_IN_FOX0 + D_FOX] * s)
    put(COL_FK, w[:, W_IN_FOX0 + D_FOX:W_IN_GATES1])


def _rearranged_w_in(w_in):
    depth, d, n_in = w_in.shape
    tr = 256
    big = pl.pallas_call(
        _wprep_kernel, grid=(depth, d // tr),
        in_specs=[pl.BlockSpec((1, tr, n_in), lambda l, r: (l, r, 0))],
        out_specs=pl.BlockSpec((1, tr, N_PROJ), lambda l, r: (l, r, 0)),
        out_shape=jax.ShapeDtypeStruct((depth, d, N_PROJ), BF16),
        compiler_params=_cparams(2), name="w_in_prep",
    )(w_in)
    gate = jnp.concatenate([w_in[..., W_IN_GATES0:W_IN_FOX0],
                            w_in[..., W_IN_GATES1:W_IN_GATES1 + FOX_HEADS]], axis=-1)
    gate = jnp.pad(gate, ((0, 0), (0, 0), (0, LANES - gate.shape[-1])))
    return big, gate


def _lane_pad(v):
    return jnp.pad(v, ((0, 0), (0, LANES - v.shape[-1])))


def kernel(x, c, ada_w, ada_b, norm1_g, w_in, ret_gn_g, mlstm_conv_w, mlstm_conv_b, mlstm_wq, mlstm_wk,
           mlstm_i_b, mlstm_f_b, mlstm_gn_g, fox_f_b, w_out, norm2_g, router_group_w, router_group_b,
           router_expert_w, router_expert_b, moe_w1, moe_w3, moe_w2, final_g):
    batch, seq, d = x.shape
    depth = ada_w.shape[0]
    t = batch * seq
    n_tiles = (2 * t) // MOE_TILE + N_EXPERTS

    mod = _modulation(c, ada_w, ada_b)
    mod = mod.reshape(depth, batch, N_MOD, 1, d)
    w_big, w_gate = _rearranged_w_in(w_in)
    gate_bias = _lane_pad(jnp.concatenate([mlstm_i_b, mlstm_f_b, fox_f_b], axis=-1))
    w_route = jnp.concatenate([router_expert_w, router_group_w], axis=-1)
    w_route = jnp.pad(w_route, ((0, 0), (0, 0), (0, LANES - w_route.shape[-1])))
    route_bias = _lane_pad(jnp.concatenate([router_expert_b, router_group_b], axis=-1))
    w_out_b = w_out.astype(BF16)
    ret_consts = _retention_consts(seq)

    xf = x.reshape(t, d)
    moe_out, tok_gates, g2_prev = None, (), None
    for l in range(depth):
        sh1, sc1, g1, sh2, sc2, g2 = (mod[l, :, k] for k in range(N_MOD))
        if l == 0:
            h1, gates = _norm_call(xf, (), (), None, norm1_g[l], sc1, sh1, w_gate[l], gate_bias[l:l + 1],
                                   seq=seq, write_x=False, side="x1", h_dtype=BF16)
        else:
            xf, h1, gates = _norm_call(xf, (moe_out, moe_out), tok_gates, g2_prev, norm1_g[l], sc1, sh1,
                                       w_gate[l], gate_bias[l:l + 1], seq=seq, write_x=True, side="x1",
                                       h_dtype=BF16, add_row_offsets=(0, t), add_tm=True)
        proj = _matmul(h1, w_big, l, BF16)
        y_ret = _retention(proj, ret_gn_g[l], ret_consts, batch=batch, seq=seq)
        y_m = _mlstm(proj, gates, mlstm_conv_w[l], mlstm_conv_b[l], mlstm_wq[l], mlstm_wk[l],
                     mlstm_gn_g[l], batch=batch, seq=seq)
        fcol, frow = _forget_cumsum(gates, batch=batch, seq=seq)
        y_f = _fox_attention(proj, fcol, frow, batch=batch, seq=seq)
        mix = _out_proj(y_ret, y_m, y_f, w_out_b, l, BF16)
        xf, h2, logits = _norm_call(xf, (mix,), (), g1, norm2_g[l], sc2, sh2, w_route[l],
                                    route_bias[l:l + 1], seq=seq, write_x=True, side="x3", h_dtype=F32,
                                    h_tm=True)
        eid, gate, rank, counts = _route(logits)
        tile_expert, n_active, row_id = _dispatch_plan(eid, rank, counts, n_tokens=t, n_tiles=n_tiles)
        moe_out = _moe_experts(h2, moe_w1, moe_w3, moe_w2, l, tile_expert, n_active, row_id,
                               n_tiles=n_tiles)
        tok_gates = tuple(jnp.broadcast_to(gate[k][:, None], (t, LANES)) for k in range(2))
        g2_prev = g2
    zeros = jnp.zeros((batch, 1, d), F32)
    (out,) = _norm_call(xf, (moe_out, moe_out), tok_gates, g2_prev, final_g, zeros, zeros, None, None,
                        seq=seq, write_x=False, side=None, h_dtype=F32,
                        add_row_offsets=(0, t), add_tm=True)
    return out.reshape(batch, seq, d)
```

```python
import functools

import jax
import jax.numpy as jnp
from jax import lax
from jax.experimental import pallas as pl
from jax.experimental.pallas import tpu as pltpu

F32 = jnp.float32
BF16 = jnp.bfloat16

HEAD_DIM = 128
LANES = 128
SUBLANES = 8
RET_HEADS = 4
MLSTM_HEADS = 4
FOX_HEADS = 8
D_RET = RET_HEADS * HEAD_DIM
D_MLSTM = MLSTM_HEADS * HEAD_DIM
D_FOX = FOX_HEADS * HEAD_DIM
CHUNK = 128
CONV_WIDTH = 4
ROPE_BASE = 10000.0
N_GROUPS = 4
EXPERTS_PER_GROUP = 8
N_EXPERTS = N_GROUPS * EXPERTS_PER_GROUP
N_MOD = 6
EPS = 1e-6
NEG_INF = float("-inf")

COL_RQ, COL_RK, COL_RV, COL_RG = 0, 512, 1024, 1536
COL_MX, COL_MV, COL_MO = 2048, 2560, 3072
COL_FQ, COL_FK, COL_FV = 3584, 4608, 5632
N_PROJ = 6656
LANE_MI, LANE_MF, LANE_FF = 0, 4, 8
LANE_RE, LANE_RG = 0, 32

VMEM_LIMIT = 56 * 1024 * 1024
MOE_TILE = 512


def _cparams(n_grid):
    return pltpu.CompilerParams(dimension_semantics=("arbitrary",) * n_grid,
                                vmem_limit_bytes=VMEM_LIMIT)


def _dot(a, b):
    return jnp.dot(a, b, preferred_element_type=F32)


def _dot_nt(a, b):
    return lax.dot_general(a, b, (((1,), (1,)), ((), ())), preferred_element_type=F32)


def _silu(x):
    return x * jax.nn.sigmoid(x)


def _log_sigmoid(x):
    return jnp.minimum(x, 0.0) - jnp.log(1.0 + jnp.exp(-jnp.abs(x)))


def _split3(x):
    hi = x.astype(BF16)
    r1 = x - hi.astype(F32)
    mid = r1.astype(BF16)
    lo = (r1 - mid.astype(F32)).astype(BF16)
    return hi, mid, lo


def _cumsum_rows(tri, x):
    hi, mid, lo = _split3(x)
    return _dot(tri, hi) + _dot(tri, mid) + _dot(tri, lo)


def _lower_tri(n, dtype):
    r = lax.broadcasted_iota(jnp.int32, (n, n), 0)
    c = lax.broadcasted_iota(jnp.int32, (n, n), 1)
    return jnp.where(c <= r, 1.0, 0.0).astype(dtype)


def _group_norm(y, gain):
    mu = jnp.mean(y, axis=-1, keepdims=True)
    yc = y - mu
    var = jnp.mean(yc * yc, axis=-1, keepdims=True)
    return yc * lax.rsqrt(var + EPS) * gain


def _mod_kernel(c_ref, w_ref, b_ref, o_ref, *, nb, tn):
    w = w_ref[0]
    rows = []
    for b in range(nb):
        cb = _silu(c_ref[b])
        parts = [jnp.sum(w[:, j * LANES:(j + 1) * LANES] * cb, axis=0, keepdims=True)
                 for j in range(tn // LANES)]
        rows.append(jnp.concatenate(parts, axis=1))
    o_ref[0] = jnp.concatenate(rows, axis=0) + b_ref[0]


def _modulation(c, ada_w, ada_b):
    depth, d, n = ada_w.shape
    nb = c.shape[0]
    tn = 512
    c_b = jnp.broadcast_to(c[:, :, None], (nb, d, LANES))
    return pl.pallas_call(
        functools.partial(_mod_kernel, nb=nb, tn=tn),
        grid=(depth, n // tn),
        in_specs=[pl.BlockSpec((nb, d, LANES), lambda l, j: (0, 0, 0)),
                  pl.BlockSpec((1, d, tn), lambda l, j: (l, 0, j)),
                  pl.BlockSpec((1, 1, tn), lambda l, j: (l, 0, j))],
        out_specs=pl.BlockSpec((1, nb, tn), lambda l, j: (l, 0, j)),
        out_shape=jax.ShapeDtypeStruct((depth, nb, n), F32),
        compiler_params=_cparams(2),
        name="adaln_mod",
    )(c_b, ada_w, ada_b.reshape(depth, 1, n))


def _from_token_major(ref, rows, nch):
    return jnp.concatenate([ref[pl.ds(c, rows, stride=nch), :] for c in range(nch)], axis=1)


def _to_token_major(ref, val, nch):
    rows = val.shape[0]
    for c in range(nch):
        ref[pl.ds(c, rows, stride=nch), :] = val[:, c * LANES:(c + 1) * LANES]


def _norm_kernel(*refs, n_add, tok_gate, write_x, side, h_dtype, add_tm, h_tm):
    it = iter(refs)
    x_ref = next(it)
    add_refs = [next(it) for _ in range(n_add)]
    tg_refs = [next(it) for _ in range(n_add)] if tok_gate else []
    gv_ref = next(it) if n_add else None
    g_ref, sc_ref, sh_ref = next(it), next(it), next(it)
    if side:
        ws_hi_ref, ws_lo_ref, bs_ref = next(it), next(it), next(it)
    xo_ref = next(it) if write_x else None
    h_ref = next(it)
    s_ref = next(it) if side else None

    x = x_ref[...]
    tm, d = x.shape
    nch = d // LANES
    if n_add:
        load = (lambda a: _from_token_major(a, tm, nch)) if add_tm else (lambda a: a[...].astype(F32))
        if tok_gate:
            upd = sum(load(a) * tg[:, 0:1] for a, tg in zip(add_refs, tg_refs))
        else:
            upd = sum(load(a) for a in add_refs)
        x = x + gv_ref[0] * upd
    if write_x:
        xo_ref[...] = x
    y = x * lax.rsqrt(jnp.mean(x * x, axis=-1, keepdims=True) + EPS)
    h = (y * g_ref[...]) * (1.0 + sc_ref[0]) + sh_ref[0]
    if h_tm:
        _to_token_major(h_ref, h.astype(h_dtype), nch)
    else:
        h_ref[...] = h.astype(h_dtype)
    if side:
        h_hi = h.astype(BF16)
        s = _dot(h_hi, ws_hi_ref[...])
        if side == "x3":
            h_lo = (h - h_hi.astype(F32)).astype(BF16)
            s = s + _dot(h_hi, ws_lo_ref[...]) + _dot(h_lo, ws_hi_ref[...])
        s_ref[...] = s + bs_ref[...]


def _norm_call(x, adds, tok_gates, gate_vec, g, sc, sh, side_w, side_b, *, seq, write_x, side,
               h_dtype, add_row_offsets=None, add_tm=False, h_tm=False):
    t, d = x.shape
    tm = 256
    nch = d // LANES
    nsb = seq // tm
    n_add = len(adds)
    tok_gate = bool(tok_gates)
    row = lambda i: (i, 0)
    vec = lambda i: (i // nsb, 0, 0)
    args, specs = [x], [pl.BlockSpec((tm, d), row)]
    for k, a in enumerate(adds):
        off = 0 if add_row_offsets is None else add_row_offsets[k] // tm
        args.append(a)
        blk = (tm * nch, LANES) if add_tm else (tm, d)
        specs.append(pl.BlockSpec(blk, lambda i, off=off: (i + off, 0)))
    for tg in tok_gates:
        args.append(tg)
        specs.append(pl.BlockSpec((tm, LANES), row))
    if n_add:
        args.append(gate_vec)
        specs.append(pl.BlockSpec((1, 1, d), vec))
    args += [g.reshape(1, d), sc, sh]
    specs += [pl.BlockSpec((1, d), lambda i: (0, 0)), pl.BlockSpec((1, 1, d), vec),
              pl.BlockSpec((1, 1, d), vec)]
    if side:
        w_hi = side_w.astype(BF16)
        w_lo = (side_w - w_hi.astype(F32)).astype(BF16)
        args += [w_hi, w_lo, side_b]
        specs += [pl.BlockSpec((d, LANES), lambda i: (0, 0)), pl.BlockSpec((d, LANES), lambda i: (0, 0)),
                  pl.BlockSpec((1, LANES), lambda i: (0, 0))]
    out_shape, out_specs = [], []
    if write_x:
        out_shape.append(jax.ShapeDtypeStruct((t, d), F32))
        out_specs.append(pl.BlockSpec((tm, d), row))
    if h_tm:
        out_shape.append(jax.ShapeDtypeStruct((t * nch, LANES), h_dtype))
        out_specs.append(pl.BlockSpec((tm * nch, LANES), row))
    else:
        out_shape.append(jax.ShapeDtypeStruct((t, d), h_dtype))
        out_specs.append(pl.BlockSpec((tm, d), row))
    if side:
        out_shape.append(jax.ShapeDtypeStruct((t, LANES), F32))
        out_specs.append(pl.BlockSpec((tm, LANES), row))
    return pl.pallas_call(
        functools.partial(_norm_kernel, n_add=n_add, tok_gate=tok_gate, write_x=write_x, side=side,
                          h_dtype=h_dtype, add_tm=add_tm, h_tm=h_tm),
        grid=(t // tm,), in_specs=specs, out_specs=out_specs, out_shape=out_shape,
        compiler_params=_cparams(1), name="resid_norm",
    )(*args)


def _mm_kernel(a_ref, w_ref, o_ref):
    o_ref[...] = _dot(a_ref[...], w_ref[0]).astype(o_ref.dtype)


def _matmul(a, w_stack, layer, out_dtype, tm=1024, tn=512):
    m, k = a.shape
    n = w_stack.shape[2]
    tm = min(tm, m)
    return pl.pallas_call(
        _mm_kernel, grid=(m // tm, n // tn),
        in_specs=[pl.BlockSpec((tm, k), lambda i, j: (i, 0)),
                  pl.BlockSpec((1, k, tn), lambda i, j: (layer, 0, j))],
        out_specs=pl.BlockSpec((tm, tn), lambda i, j: (i, j)),
        out_shape=jax.ShapeDtypeStruct((m, n), out_dtype),
        compiler_params=_cparams(2), name="in_proj",
    )(a, w_stack)


def _mm3_kernel(a1_ref, a2_ref, a3_ref, w_ref, o_ref):
    k1, k2 = a1_ref.shape[1], a2_ref.shape[1]
    acc = _dot(a1_ref[...], w_ref[0, 0:k1, :])
    acc = acc + _dot(a2_ref[...], w_ref[0, k1:k1 + k2, :])
    acc = acc + _dot(a3_ref[...], w_ref[0, k1 + k2:, :])
    o_ref[...] = acc.astype(o_ref.dtype)


def _out_proj(y_ret, y_m, y_f, w_stack, layer, out_dtype, tm=1024, tn=512):
    m = y_ret.shape[0]
    _, k, n = w_stack.shape
    tm = min(tm, m)
    return pl.pallas_call(
        _mm3_kernel, grid=(m // tm, n // tn),
        in_specs=[pl.BlockSpec((tm, y_ret.shape[1]), lambda i, j: (i, 0)),
                  pl.BlockSpec((tm, y_m.shape[1]), lambda i, j: (i, 0)),
                  pl.BlockSpec((tm, y_f.shape[1]), lambda i, j: (i, 0)),
                  pl.BlockSpec((1, k, tn), lambda i, j: (layer, 0, j))],
        out_specs=pl.BlockSpec((tm, tn), lambda i, j: (i, j)),
        out_shape=jax.ShapeDtypeStruct((m, n), out_dtype),
        compiler_params=_cparams(2), name="out_proj",
    )(y_ret, y_m, y_f, w_stack)


def _ret_kernel(q_ref, k_ref, v_ref, g_ref, cos_ref, sin_ref, decay_ref, zeta_ref, xi_ref, gam_ref,
                gn_ref, o_ref, state_ref, *, ts):
    @pl.when(pl.program_id(1) == 0)
    def _():
        state_ref[...] = jnp.zeros_like(state_ref)

    for c in range(ts // CHUNK):
        rows = slice(c * CHUNK, (c + 1) * CHUNK)
        cosv, sinv = cos_ref[rows, :], sin_ref[rows, :]
        for h in range(RET_HEADS):
            cols = slice(h * HEAD_DIM, (h + 1) * HEAD_DIM)
            q = q_ref[rows, cols].astype(F32)
            k = k_ref[rows, cols].astype(F32)
            qr = q * cosv + pltpu.roll(q, HEAD_DIM // 2, 1) * sinv
            kr = k * cosv + pltpu.roll(k, HEAD_DIM // 2, 1) * sinv
            qb, kb, vb = qr.astype(BF16), kr.astype(BF16), v_ref[rows, cols]
            s = _dot_nt(qb, kb) * decay_ref[h]
            state = state_ref[h]
            y = _dot(s.astype(BF16), vb) + _dot(qb, state.astype(BF16)) * xi_ref[h]
            kz_t = (kr * zeta_ref[h]).T.astype(BF16)
            state_ref[h] = state * gam_ref[h] + _dot(kz_t, vb)
            out = _group_norm(y, gn_ref[:, cols]) * _silu(g_ref[rows, cols].astype(F32))
            o_ref[rows, cols] = out.astype(o_ref.dtype)


def _retention(proj, gn_g, consts, *, batch, seq):
    ts = 512
    ns = seq // ts
    cosf, sins, decay, zeta, xi, gam = consts
    rowblk = lambda cb: pl.BlockSpec((ts, D_RET), lambda b, s, cb=cb: (b * ns + s, cb))
    full3 = pl.BlockSpec((RET_HEADS, CHUNK, HEAD_DIM), lambda b, s: (0, 0, 0))
    return pl.pallas_call(
        functools.partial(_ret_kernel, ts=ts), grid=(batch, ns),
        in_specs=[rowblk(COL_RQ // D_RET), rowblk(COL_RK // D_RET), rowblk(COL_RV // D_RET),
                  rowblk(COL_RG // D_RET),
                  pl.BlockSpec((ts, HEAD_DIM), lambda b, s: (s, 0)),
                  pl.BlockSpec((ts, HEAD_DIM), lambda b, s: (s, 0)),
                  full3, full3, full3,
                  pl.BlockSpec((RET_HEADS, 1, HEAD_DIM), lambda b, s: (0, 0, 0)),
                  pl.BlockSpec((1, D_RET), lambda b, s: (0, 0))],
        out_specs=pl.BlockSpec((ts, D_RET), lambda b, s: (b * ns + s, 0)),
        out_shape=jax.ShapeDtypeStruct((batch * seq, D_RET), BF16),
        scratch_shapes=[pltpu.VMEM((RET_HEADS, HEAD_DIM, HEAD_DIM), F32)],
        compiler_params=_cparams(2), name="retention",
    )(proj, proj, proj, proj, cosf, sins, decay, zeta, xi, gam, gn_g.reshape(1, D_RET))


def _retention_consts(seq):
    h = RET_HEADS
    log_gamma = jnp.log1p(-jnp.power(2.0, -5.0 - jnp.arange(h, dtype=F32)))
    pos = jnp.arange(CHUNK, dtype=F32)
    diff = pos[:, None] - pos[None, :]
    decay = jnp.where(diff >= 0, jnp.exp(jnp.maximum(diff, 0.0)[None] * log_gamma[:, None, None]), 0.0)
    zeta = jnp.exp((CHUNK - 1 - pos)[None, :] * log_gamma[:, None])
    xi = jnp.exp((pos + 1)[None, :] * log_gamma[:, None])
    gam = jnp.exp(CHUNK * log_gamma)
    bcast = lambda v: jnp.broadcast_to(v[:, :, None], (h, CHUNK, HEAD_DIM))
    half = HEAD_DIM // 2
    inv = ROPE_BASE ** (-jnp.arange(half, dtype=F32) / half)
    ang = jnp.arange(seq, dtype=F32)[:, None] * inv[None, :]
    cosf = jnp.concatenate([jnp.cos(ang), jnp.cos(ang)], axis=1)
    sins = jnp.concatenate([-jnp.sin(ang), jnp.sin(ang)], axis=1)
    gam_b = jnp.broadcast_to(gam[:, None, None], (h, 1, HEAD_DIM))
    return cosf, sins, decay, bcast(zeta), bcast(xi), gam_b


def _mlstm_kernel(x_ref, v_ref, og_ref, gates_ref, cw_ref, cb_ref, wq_ref, wk_ref, gn_ref, o_ref,
                  xbuf, c_state, n_state, m_state, *, ts):
    @pl.when(pl.program_id(1) == 0)
    def _():
        c_state[...] = jnp.zeros_like(c_state)
        n_state[...] = jnp.zeros_like(n_state)
        m_state[...] = jnp.full_like(m_state, NEG_INF)
        xbuf[0:8, :] = jnp.zeros((8, D_MLSTM), F32)

    x = x_ref[...].astype(F32)
    xbuf[8:ts + 8, :] = x
    conv = cb_ref[...]
    for j in range(CONV_WIDTH):
        start = 8 - (CONV_WIDTH - 1) + j
        conv = conv + cw_ref[j:j + 1, :] * xbuf[start:start + ts, :]
    xbuf[0:8, :] = x[ts - 8:ts, :]
    xc = _silu(conv).astype(BF16)

    tri = _lower_tri(CHUNK, BF16)
    r_i = lax.broadcasted_iota(jnp.int32, (CHUNK, CHUNK), 0)
    c_i = lax.broadcasted_iota(jnp.int32, (CHUNK, CHUNK), 1)
    causal = c_i <= r_i

    q_all = [_dot(xc[:, h * HEAD_DIM:(h + 1) * HEAD_DIM], wq_ref[h]).astype(BF16) for h in range(MLSTM_HEADS)]
    k_all = [_dot(xc[:, h * HEAD_DIM:(h + 1) * HEAD_DIM], wk_ref[h]) for h in range(MLSTM_HEADS)]

    for c in range(ts // CHUNK):
        rows = slice(c * CHUNK, (c + 1) * CHUNK)
        gc = gates_ref[rows, :]
        gt = gc.T
        b_col = _cumsum_rows(tri, _log_sigmoid(gc))
        b_row = b_col.T
        for h in range(MLSTM_HEADS):
            cols = slice(h * HEAD_DIM, (h + 1) * HEAD_DIM)
            li_r = gt[LANE_MI + h:LANE_MI + h + 1, :]
            li_c = gc[:, LANE_MI + h:LANE_MI + h + 1]
            b_c = b_col[:, LANE_MF + h:LANE_MF + h + 1]
            b_r = b_row[LANE_MF + h:LANE_MF + h + 1, :]
            b_end = b_r[:, CHUNK - 1:CHUNK]
            m_prev = m_state[h][:, 0:1]
            c_prev, n_prev = c_state[h], n_state[h]

            d_log = jnp.where(causal, b_c - b_r + li_r, NEG_INF)
            a_c = b_end - b_c + li_c
            a_max = jnp.max(a_c, axis=0, keepdims=True)
            w_c = jnp.exp(a_c - a_max)
            inter = b_c + m_prev
            m_t = jnp.maximum(jnp.max(d_log, axis=1, keepdims=True), inter)
            d_w = jnp.exp(d_log - m_t)
            inter_w = jnp.exp(inter - m_t)

            q = q_all[h][rows, :]
            kf = k_all[h][rows, :]
            k = kf.astype(BF16)
            v = v_ref[rows, cols]
            s = _dot_nt(q, k) * d_w
            num = _dot(s.astype(BF16), v) + inter_w * _dot(q, c_prev.astype(BF16))
            qn = jnp.sum(q.astype(F32) * n_prev, axis=1, keepdims=True)
            den = jnp.sum(s, axis=1, keepdims=True) + inter_w * qn
            hh = num / jnp.maximum(jnp.abs(den), jnp.exp(-m_t))

            kw = kf * w_c
            kv = _dot(kw.T.astype(BF16), v)
            ks = jnp.sum(kw, axis=0, keepdims=True)
            m_new = jnp.maximum(b_end + m_prev, a_max)
            s_old = jnp.exp(b_end + m_prev - m_new)
            s_new = jnp.exp(a_max - m_new)
            c_state[h] = s_old * c_prev + s_new * kv
            n_state[h] = s_old * n_prev + s_new * ks
            m_state[h] = jnp.broadcast_to(m_new, (1, HEAD_DIM))

            out = jax.nn.sigmoid(og_ref[rows, cols].astype(F32)) * _group_norm(hh, gn_ref[:, cols])
            o_ref[rows, cols] = out.astype(o_ref.dtype)


def _mlstm(proj, gates, conv_w, conv_b, wq, wk, gn_g, *, batch, seq):
    ts = 512
    ns = seq // ts
    rowblk = lambda cb: pl.BlockSpec((ts, D_MLSTM), lambda b, s, cb=cb: (b * ns + s, cb))
    wspec = pl.BlockSpec((MLSTM_HEADS, HEAD_DIM, HEAD_DIM), lambda b, s: (0, 0, 0))
    return pl.pallas_call(
        functools.partial(_mlstm_kernel, ts=ts), grid=(batch, ns),
        in_specs=[rowblk(COL_MX // D_MLSTM), rowblk(COL_MV // D_MLSTM), rowblk(COL_MO // D_MLSTM),
                  pl.BlockSpec((ts, LANES), lambda b, s: (b * ns + s, 0)),
                  pl.BlockSpec((CONV_WIDTH, D_MLSTM), lambda b, s: (0, 0)),
                  pl.BlockSpec((1, D_MLSTM), lambda b, s: (0, 0)),
                  wspec, wspec,
                  pl.BlockSpec((1, D_MLSTM), lambda b, s: (0, 0))],
        out_specs=pl.BlockSpec((ts, D_MLSTM), lambda b, s: (b * ns + s, 0)),
        out_shape=jax.ShapeDtypeStruct((batch * seq, D_MLSTM), BF16),
        scratch_shapes=[pltpu.VMEM((ts + 8, D_MLSTM), F32),
                        pltpu.VMEM((MLSTM_HEADS, HEAD_DIM, HEAD_DIM), F32),
                        pltpu.VMEM((MLSTM_HEADS, 1, HEAD_DIM), F32),
                        pltpu.VMEM((MLSTM_HEADS, 1, HEAD_DIM), F32)],
        compiler_params=_cparams(2), name="mlstm",
    )(proj, proj, proj, gates, conv_w, conv_b.reshape(1, D_MLSTM), wq.astype(BF16),
      (wk * HEAD_DIM ** -0.5).astype(BF16), gn_g.reshape(1, D_MLSTM))


def _fcum_kernel(gates_ref, fcol_ref, frow_ref, carry_ref, *, ts):
    @pl.when(pl.program_id(1) == 0)
    def _():
        carry_ref[...] = jnp.zeros_like(carry_ref)

    tri = _lower_tri(CHUNK, BF16)
    carry = carry_ref[...]
    for c in range(ts // CHUNK):
        rows = slice(c * CHUNK, (c + 1) * CHUNK)
        cum = _cumsum_rows(tri, _log_sigmoid(gates_ref[rows, :])) + carry
        carry = cum[CHUNK - 1:CHUNK, :]
        fcol_ref[rows, :] = cum
        frow_ref[0, :, rows] = cum.T[LANE_FF:LANE_FF + FOX_HEADS, :]
    carry_ref[...] = carry


def _forget_cumsum(gates, *, batch, seq):
    ts = 512
    ns = seq // ts
    return pl.pallas_call(
        functools.partial(_fcum_kernel, ts=ts), grid=(batch, ns),
        in_specs=[pl.BlockSpec((ts, LANES), lambda b, s: (b * ns + s, 0))],
        out_specs=[pl.BlockSpec((ts, LANES), lambda b, s: (b * ns + s, 0)),
                   pl.BlockSpec((1, FOX_HEADS, ts), lambda b, s: (b, 0, s))],
        out_shape=[jax.ShapeDtypeStruct((batch * seq, LANES), F32),
                   jax.ShapeDtypeStruct((batch, FOX_HEADS, seq), F32)],
        scratch_shapes=[pltpu.VMEM((1, LANES), F32)],
        compiler_params=_cparams(2), name="forget_cumsum",
    )(gates)


def _fox_kernel(q_ref, k_ref, v_ref, fcol_ref, frow_ref, o_ref, s_ref, acc_ref, m_ref, l_ref, *, tq):
    h = pl.program_id(1)
    i = pl.program_id(2)
    q = q_ref[...]
    tk = 2 * tq
    lane = lax.broadcasted_iota(jnp.int32, (tq, LANES), 1)
    f_q = jnp.sum(jnp.where(lane == LANE_FF + h, fcol_ref[...], 0.0), axis=1, keepdims=True)
    f_q = jnp.broadcast_to(f_q, (tq, LANES))
    lane_tiles = lambda a: [a[:, c * LANES:(c + 1) * LANES] for c in range(a.shape[1] // LANES)]

    def scores(start, width):
        return _dot_nt(q, k_ref[pl.ds(start, width), :]) - frow_ref[0, :, pl.ds(start, width)]

    def keep(start, s):
        s_ref[:, pl.ds(start, s.shape[1])] = s
        m = m_ref[...]
        for s_c in lane_tiles(s):
            m = jnp.maximum(m, s_c)
        m_ref[...] = m

    def weigh(start, width):
        ps = [jnp.exp(s_c + shift) for s_c in lane_tiles(s_ref[:, pl.ds(start, width)])]
        l_ref[...] = l_ref[...] + sum(ps[1:], ps[0])
        p = jnp.concatenate(ps, axis=1).astype(BF16)
        acc_ref[...] = acc_ref[...] + _dot(p, v_ref[pl.ds(start, width), :])

    def sweep(n_keys, fn):
        n_wide = n_keys // tk

        def body(j, carry):
            fn(pl.multiple_of(j * tk, tk), tk)
            return carry

        lax.fori_loop(0, n_wide, body, 0)

        @pl.when(n_wide * tk < n_keys)
        def _():
            fn(pl.multiple_of(n_wide * tk, tq), tq)

    m_ref[...] = jnp.full_like(m_ref, NEG_INF)
    diag = pl.multiple_of(i * tq, tq)
    sweep(diag, lambda start, width: keep(start, scores(start, width)))
    r_i = lax.broadcasted_iota(jnp.int32, (tq, tq), 0)
    c_i = lax.broadcasted_iota(jnp.int32, (tq, tq), 1)
    keep(diag, jnp.where(c_i <= r_i, scores(diag, tq), NEG_INF))

    m_row = jnp.max(m_ref[...], axis=1, keepdims=True) + f_q
    shift = f_q - m_row
    acc_ref[...] = jnp.zeros_like(acc_ref)
    l_ref[...] = jnp.zeros_like(l_ref)
    sweep(diag + tq, weigh)
    l_row = jnp.sum(l_ref[...], axis=1, keepdims=True)
    o_ref[...] = (acc_ref[...] / l_row).astype(o_ref.dtype)


def _fox_attention(proj, fcol, frow, *, batch, seq):
    tq = 512
    nq = seq // tq
    cq, ck, cv = COL_FQ // HEAD_DIM, COL_FK // HEAD_DIM, COL_FV // HEAD_DIM
    return pl.pallas_call(
        functools.partial(_fox_kernel, tq=tq), grid=(batch, FOX_HEADS, nq),
        in_specs=[pl.BlockSpec((tq, HEAD_DIM), lambda b, h, i: (b * nq + i, cq + h)),
                  pl.BlockSpec((seq, HEAD_DIM), lambda b, h, i: (b, ck + h)),
                  pl.BlockSpec((seq, HEAD_DIM), lambda b, h, i: (b, cv + h)),
                  pl.BlockSpec((tq, LANES), lambda b, h, i: (b * nq + i, 0)),
                  pl.BlockSpec((1, 1, seq), lambda b, h, i: (b * FOX_HEADS + h, 0, 0))],
        out_specs=pl.BlockSpec((tq, HEAD_DIM), lambda b, h, i: (b * nq + i, h)),
        out_shape=jax.ShapeDtypeStruct((batch * seq, D_FOX), BF16),
        scratch_shapes=[pltpu.VMEM((tq, seq), F32), pltpu.VMEM((tq, HEAD_DIM), F32),
                        pltpu.VMEM((tq, LANES), F32), pltpu.VMEM((tq, LANES), F32)],
        compiler_params=_cparams(3), name="fox_attention",
    )(proj, proj, proj, fcol, frow.reshape(batch * FOX_HEADS, 1, seq))


def _route_kernel(logit_ref, eid_ref, gate_ref, rank_ref, cnt_ref, upper_ref, base_ref, *, tm):
    @pl.when(pl.program_id(0) == 0)
    def _():
        r = lax.broadcasted_iota(jnp.int32, (tm, tm), 0)
        c = lax.broadcasted_iota(jnp.int32, (tm, tm), 1)
        upper_ref[...] = jnp.where(r < c, 1.0, 0.0).astype(BF16)
        base_ref[...] = jnp.zeros_like(base_ref)

    lt = logit_ref[...].T
    e = lt[LANE_RE:LANE_RE + N_EXPERTS, :]
    g = lt[LANE_RG:LANE_RG + N_GROUPS, :]
    g_max = jnp.max(g, axis=0, keepdims=True)
    g_val = 1.0 / jnp.sum(jnp.exp(g - g_max), axis=0, keepdims=True)
    g_row = lax.broadcasted_iota(jnp.int32, (N_GROUPS, tm), 0).astype(F32)
    g_idx = jnp.min(jnp.where(g == g_max, g_row, float(N_GROUPS)), axis=0, keepdims=True)

    e_row_i = lax.broadcasted_iota(jnp.int32, (N_EXPERTS, tm), 0)
    e_row = e_row_i.astype(F32)
    e_grp = jnp.right_shift(e_row_i, EXPERTS_PER_GROUP.bit_length() - 1).astype(F32)
    el = jnp.where(e_grp == g_idx, e, NEG_INF)
    max1 = jnp.max(el, axis=0, keepdims=True)
    idx1 = jnp.min(jnp.where(el == max1, e_row, float(N_EXPERTS)), axis=0, keepdims=True)
    el2 = jnp.where(e_row == idx1, NEG_INF, el)
    max2 = jnp.max(el2, axis=0, keepdims=True)
    idx2 = jnp.min(jnp.where(el2 == max2, e_row, float(N_EXPERTS)), axis=0, keepdims=True)
    e_sum = jnp.sum(jnp.exp(el - max1), axis=0, keepdims=True)
    p1 = 1.0 / e_sum
    p2 = jnp.exp(max2 - max1) / e_sum
    p_sum = p1 + p2
    gate_ref[0:1, :] = g_val * (p1 / p_sum)
    gate_ref[1:2, :] = g_val * (p2 / p_sum)
    eid_ref[0:1, :] = idx1.astype(jnp.int32)
    eid_ref[1:2, :] = idx2.astype(jnp.int32)

    oh1 = e_row == idx1
    oh2 = e_row == idx2
    onehot = jnp.where(oh1 | oh2, 1.0, 0.0)
    before = _dot(onehot.astype(BF16), upper_ref[...]) + base_ref[:, 0:1]
    rank_ref[0:1, :] = jnp.sum(jnp.where(oh1, before, 0.0), axis=0, keepdims=True).astype(jnp.int32)
    rank_ref[1:2, :] = jnp.sum(jnp.where(oh2, before, 0.0), axis=0, keepdims=True).astype(jnp.int32)
    base_ref[...] = base_ref[...] + jnp.sum(onehot, axis=1, keepdims=True)
    cnt_ref[...] = base_ref[...]


def _route(logits):
    t = logits.shape[0]
    tm = min(1024, t)
    tok = lambda i: (0, i)
    return pl.pallas_call(
        functools.partial(_route_kernel, tm=tm), grid=(t // tm,),
        in_specs=[pl.BlockSpec((tm, LANES), lambda i: (i, 0))],
        out_specs=[pl.BlockSpec((2, tm), tok), pl.BlockSpec((2, tm), tok), pl.BlockSpec((2, tm), tok),
                   pl.BlockSpec((N_EXPERTS, LANES), lambda i: (0, 0))],
        out_shape=[jax.ShapeDtypeStruct((2, t), jnp.int32), jax.ShapeDtypeStruct((2, t), F32),
                   jax.ShapeDtypeStruct((2, t), jnp.int32),
                   jax.ShapeDtypeStruct((N_EXPERTS, LANES), F32)],
        scratch_shapes=[pltpu.VMEM((tm, tm), BF16), pltpu.VMEM((N_EXPERTS, LANES), F32)],
        compiler_params=_cparams(1), name="route",
    )(logits)


def _moe_kernel(tile_expert, n_active, row_id, h_hbm, w1_ref, w3_ref, w2_ref, o_hbm,
                xbuf, xb, ybuf, gsem, ssem, *, tm, nch, n_tokens, n_tiles):
    del tile_expert
    i = pl.program_id(0)
    n_act = n_active[0]
    last_step = jnp.minimum(n_act, n_tiles - 1)

    def token_of(rid):
        if n_tokens & (n_tokens - 1) == 0:
            return jnp.bitwise_and(rid, n_tokens - 1)
        return rid - n_tokens * ((rid >= n_tokens).astype(jnp.int32) + (rid >= 2 * n_tokens).astype(jnp.int32))

    def hbm_token(ref, tok):
        return ref.at[pl.ds(pl.multiple_of(tok * nch, nch), nch)]

    def vmem_row(ref, r):
        if isinstance(r, int):
            return ref.at[r // SUBLANES, :, r % SUBLANES, :]
        return ref.at[r // SUBLANES, :, lax.rem(r, SUBLANES), :]

    def start_gather(tile, r):
        tok = token_of(row_id[(tile + 1) * tm + r])
        pltpu.make_async_copy(hbm_token(h_hbm, tok), vmem_row(xbuf, r), gsem).start()

    def start_scatter(tile, r):
        dst = row_id[(tile + 1) * tm + r]
        pltpu.make_async_copy(vmem_row(ybuf, r), hbm_token(o_hbm, dst), ssem).start()

    def wait_rows(sem):
        for _ in range(tm):
            pltpu.make_async_copy(hbm_token(h_hbm, 0), vmem_row(xbuf, 0), sem).wait()

    def rows_loop(fn):
        def body(r, carry):
            fn(r)
            return carry
        lax.fori_loop(0, tm, body, 0, unroll=8)

    @pl.when(i <= last_step)
    def _():
        @pl.when(i == 0)
        def _():
            ybuf[...] = jnp.zeros_like(ybuf)
            rows_loop(lambda r: start_gather(0, r))

        wait_rows(gsem)
        xb[...] = jnp.concatenate([xbuf[:, c].reshape(tm, LANES) for c in range(nch)], axis=1).astype(BF16)
        nxt = jnp.minimum(i + 1, n_tiles - 1)
        for r in range(tm):
            start_gather(nxt, r)
            start_scatter(i - 1, r)

        @pl.when(n_act >= i)
        def _():
            x = xb[...]
            h1 = _dot(x, w1_ref[0, 0].astype(BF16))
            h3 = _dot(x, w3_ref[0, 0].astype(BF16))
            act = (_silu(h1) * h3).astype(BF16)
            y = _dot(act, w2_ref[0, 0].astype(BF16))
            wait_rows(ssem)
            for c in range(nch):
                ybuf[:, c] = y[:, c * LANES:(c + 1) * LANES].reshape(tm // SUBLANES, SUBLANES, LANES)

        @pl.when(i == last_step)
        def _():
            wait_rows(gsem)

            @pl.when(i < n_act)
            def _():
                rows_loop(lambda r: start_scatter(i, r))
                wait_rows(ssem)


def _moe_experts(h2_tm, w1, w3, w2, layer, tile_expert, n_active, row_id, *, n_tiles):
    _, _, d, de = w1.shape
    nch = d // LANES
    t = h2_tm.shape[0] // nch
    tm = MOE_TILE
    wmap = lambda i, te, na, rid: (layer, te[i], 0, 0)
    grid_spec = pltpu.PrefetchScalarGridSpec(
        num_scalar_prefetch=3, grid=(n_tiles,),
        in_specs=[pl.BlockSpec(memory_space=pl.ANY),
                  pl.BlockSpec((1, 1, d, de), wmap), pl.BlockSpec((1, 1, d, de), wmap),
                  pl.BlockSpec((1, 1, de, d), wmap)],
        out_specs=pl.BlockSpec(memory_space=pl.ANY),
        scratch_shapes=[pltpu.VMEM((tm // SUBLANES, nch, SUBLANES, LANES), F32), pltpu.VMEM((tm, d), BF16),
                        pltpu.VMEM((tm // SUBLANES, nch, SUBLANES, LANES), F32),
                        pltpu.SemaphoreType.DMA(()), pltpu.SemaphoreType.DMA(())])
    return pl.pallas_call(
        functools.partial(_moe_kernel, tm=tm, nch=nch, n_tokens=t, n_tiles=n_tiles), grid_spec=grid_spec,
        out_shape=jax.ShapeDtypeStruct(((2 * t + tm) * nch, LANES), F32),
        compiler_params=_cparams(1), name="moe_experts",
    )(tile_expert, n_active, row_id, h2_tm, w1, w3, w2)


def _dispatch_plan(eid, rank, counts, *, n_tokens, n_tiles):
    tm = MOE_TILE
    cnt = counts[:, 0].astype(jnp.int32)
    tiles_e = (cnt + tm - 1) // tm
    tile_end = jnp.cumsum(tiles_e)
    tile_start = tile_end - tiles_e
    experts = jnp.arange(N_EXPERTS, dtype=jnp.int32)
    row_off = jnp.sum(jnp.where(eid[..., None] == experts, tile_start * tm, 0), axis=-1)
    dest = row_off + rank
    n_active = tile_end[-1]
    tile_ids = jnp.arange(n_tiles, dtype=jnp.int32)
    last_tile = jnp.minimum(tile_ids, n_active - 1)
    tile_expert = jnp.sum((tile_end[None, :] <= last_tile[:, None]).astype(jnp.int32), axis=-1)
    tile_expert = jnp.minimum(tile_expert, N_EXPERTS - 1)
    slot_row = (jnp.arange(n_tokens, dtype=jnp.int32)[None, :]
                + jnp.arange(2, dtype=jnp.int32)[:, None] * n_tokens)
    spare = 2 * n_tokens + jnp.arange((n_tiles + 1) * tm, dtype=jnp.int32) % tm
    row_id = spare.at[tm + dest.reshape(-1)].set(slot_row.reshape(-1))
    return tile_expert, n_active.reshape(1), row_id


W_IN_GATES0 = COL_MO + D_MLSTM
W_IN_FOX0 = W_IN_GATES0 + 2 * MLSTM_HEADS
W_IN_GATES1 = W_IN_FOX0 + 3 * D_FOX


def _wprep_kernel(w_ref, o_ref):
    s = HEAD_DIM ** -0.5
    w = w_ref[0]

    def put(dst, src):
        o_ref[0, :, dst:dst + src.shape[1]] = src.astype(BF16)

    put(COL_RQ, w[:, 0:COL_RK])
    put(COL_RK, w[:, COL_RK:COL_RV] * s)
    put(COL_RV, w[:, COL_RV:W_IN_GATES0])
    put(COL_FQ, w[:, W_IN_FOX0:W_IN_FOX0 + D_FOX] * s)
    put(COL_FK, w[:, W_IN_FOX0 + D_FOX:W_IN_GATES1])


def _rearranged_w_in(w_in):
    depth, d, n_in = w_in.shape
    tr = 256
    big = pl.pallas_call(
        _wprep_kernel, grid=(depth, d // tr),
        in_specs=[pl.BlockSpec((1, tr, n_in), lambda l, r: (l, r, 0))],
        out_specs=pl.BlockSpec((1, tr, N_PROJ), lambda l, r: (l, r, 0)),
        out_shape=jax.ShapeDtypeStruct((depth, d, N_PROJ), BF16),
        compiler_params=_cparams(2), name="w_in_prep",
    )(w_in)
    gate = jnp.concatenate([w_in[..., W_IN_GATES0:W_IN_FOX0],
                            w_in[..., W_IN_GATES1:W_IN_GATES1 + FOX_HEADS]], axis=-1)
    gate = jnp.pad(gate, ((0, 0), (0, 0), (0, LANES - gate.shape[-1])))
    return big, gate


def _lane_pad(v):
    return jnp.pad(v, ((0, 0), (0, LANES - v.shape[-1])))


def kernel(x, c, ada_w, ada_b, norm1_g, w_in, ret_gn_g, mlstm_conv_w, mlstm_conv_b, mlstm_wq, mlstm_wk,
           mlstm_i_b, mlstm_f_b, mlstm_gn_g, fox_f_b, w_out, norm2_g, router_group_w, router_group_b,
           router_expert_w, router_expert_b, moe_w1, moe_w3, moe_w2, final_g):
    batch, seq, d = x.shape
    depth = ada_w.shape[0]
    t = batch * seq
    n_tiles = (2 * t) // MOE_TILE + N_EXPERTS

    mod = _modulation(c, ada_w, ada_b)
    mod = mod.reshape(depth, batch, N_MOD, 1, d)
    w_big, w_gate = _rearranged_w_in(w_in)
    gate_bias = _lane_pad(jnp.concatenate([mlstm_i_b, mlstm_f_b, fox_f_b], axis=-1))
    w_route = jnp.concatenate([router_expert_w, router_group_w], axis=-1)
    w_route = jnp.pad(w_route, ((0, 0), (0, 0), (0, LANES - w_route.shape[-1])))
    route_bias = _lane_pad(jnp.concatenate([router_expert_b, router_group_b], axis=-1))
    w_out_b = w_out.astype(BF16)
    ret_consts = _retention_consts(seq)

    xf = x.reshape(t, d)
    moe_out, tok_gates, g2_prev = None, (), None
    for l in range(depth):
        sh1, sc1, g1, sh2, sc2, g2 = (mod[l, :, k] for k in range(N_MOD))
        if l == 0:
            h1, gates = _norm_call(xf, (), (), None, norm1_g[l], sc1, sh1, w_gate[l], gate_bias[l:l + 1],
                                   seq=seq, write_x=False, side="x1", h_dtype=BF16)
        else:
            xf, h1, gates = _norm_call(xf, (moe_out, moe_out), tok_gates, g2_prev, norm1_g[l], sc1, sh1,
                                       w_gate[l], gate_bias[l:l + 1], seq=seq, write_x=True, side="x1",
                                       h_dtype=BF16, add_row_offsets=(0, t), add_tm=True)
        proj = _matmul(h1, w_big, l, BF16)
        y_ret = _retention(proj, ret_gn_g[l], ret_consts, batch=batch, seq=seq)
        y_m = _mlstm(proj, gates, mlstm_conv_w[l], mlstm_conv_b[l], mlstm_wq[l], mlstm_wk[l],
                     mlstm_gn_g[l], batch=batch, seq=seq)
        fcol, frow = _forget_cumsum(gates, batch=batch, seq=seq)
        y_f = _fox_attention(proj, fcol, frow, batch=batch, seq=seq)
        mix = _out_proj(y_ret, y_m, y_f, w_out_b, l, BF16)
        xf, h2, logits = _norm_call(xf, (mix,), (), g1, norm2_g[l], sc2, sh2, w_route[l],
                                    route_bias[l:l + 1], seq=seq, write_x=True, side="x3", h_dtype=F32,
                                    h_tm=True)
        eid, gate, rank, counts = _route(logits)
        tile_expert, n_active, row_id = _dispatch_plan(eid, rank, counts, n_tokens=t, n_tiles=n_tiles)
        moe_out = _moe_experts(h2, moe_w1, moe_w3, moe_w2, l, tile_expert, n_active, row_id,
                               n_tiles=n_tiles)
        tok_gates = tuple(jnp.broadcast_to(gate[k][:, None], (t, LANES)) for k in range(2))
        g2_prev = g2
    zeros = jnp.zeros((batch, 1, d), F32)
    (out,) = _norm_call(xf, (moe_out, moe_out), tok_gates, g2_prev, final_g, zeros, zeros, None, None,
                        seq=seq, write_x=False, side=None, h_dtype=F32,
                        add_row_offsets=(0, t), add_tm=True)
    return out.reshape(batch, seq, d)
```

```python
import functools

import jax
import jax.numpy as jnp
from jax import lax
from jax.experimental import pallas as pl
from jax.experimental.pallas import tpu as pltpu

F32 = jnp.float32
BF16 = jnp.bfloat16

HEAD_DIM = 128
LANES = 128
SUBLANES = 8
RET_HEADS = 4
MLSTM_HEADS = 4
FOX_HEADS = 8
D_RET = RET_HEADS * HEAD_DIM
D_MLSTM = MLSTM_HEADS * HEAD_DIM
D_FOX = FOX_HEADS * HEAD_DIM
CHUNK = 128
CONV_WIDTH = 4
ROPE_BASE = 10000.0
N_GROUPS = 4
EXPERTS_PER_GROUP = 8
N_EXPERTS = N_GROUPS * EXPERTS_PER_GROUP
N_MOD = 6
EPS = 1e-6
NEG_INF = float("-inf")

COL_RQ, COL_RK, COL_RV, COL_RG = 0, 512, 1024, 1536
COL_MX, COL_MV, COL_MO = 2048, 2560, 3072
COL_FQ, COL_FK, COL_FV = 3584, 4608, 5632
N_PROJ = 6656
LANE_MI, LANE_MF, LANE_FF = 0, 4, 8
LANE_RE, LANE_RG = 0, 32

VMEM_LIMIT = 56 * 1024 * 1024
MOE_TILE = 512


def _cparams(n_grid):
    return pltpu.CompilerParams(dimension_semantics=("arbitrary",) * n_grid,
                                vmem_limit_bytes=VMEM_LIMIT)


def _dot(a, b):
    return jnp.dot(a, b, preferred_element_type=F32)


def _dot_nt(a, b):
    return lax.dot_general(a, b, (((1,), (1,)), ((), ())), preferred_element_type=F32)


def _silu(x):
    return x * jax.nn.sigmoid(x)


def _log_sigmoid(x):
    return jnp.minimum(x, 0.0) - jnp.log(1.0 + jnp.exp(-jnp.abs(x)))


def _split3(x):
    hi = x.astype(BF16)
    r1 = x - hi.astype(F32)
    mid = r1.astype(BF16)
    lo = (r1 - mid.astype(F32)).astype(BF16)
    return hi, mid, lo


def _cumsum_rows(tri, x):
    hi, mid, lo = _split3(x)
    return _dot(tri, hi) + _dot(tri, mid) + _dot(tri, lo)


def _lower_tri(n, dtype):
    r = lax.broadcasted_iota(jnp.int32, (n, n), 0)
    c = lax.broadcasted_iota(jnp.int32, (n, n), 1)
    return jnp.where(c <= r, 1.0, 0.0).astype(dtype)


def _group_norm(y, gain):
    mu = jnp.mean(y, axis=-1, keepdims=True)
    yc = y - mu
    var = jnp.mean(yc * yc, axis=-1, keepdims=True)
    return yc * lax.rsqrt(var + EPS) * gain


def _mod_kernel(c_ref, w_ref, b_ref, o_ref, *, nb, tn):
    w = w_ref[0]
    rows = []
    for b in range(nb):
        cb = _silu(c_ref[b])
        parts = [jnp.sum(w[:, j * LANES:(j + 1) * LANES] * cb, axis=0, keepdims=True)
                 for j in range(tn // LANES)]
        rows.append(jnp.concatenate(parts, axis=1))
    o_ref[0] = jnp.concatenate(rows, axis=0) + b_ref[0]


def _modulation(c, ada_w, ada_b):
    depth, d, n = ada_w.shape
    nb = c.shape[0]
    tn = 512
    c_b = jnp.broadcast_to(c[:, :, None], (nb, d, LANES))
    return pl.pallas_call(
        functools.partial(_mod_kernel, nb=nb, tn=tn),
        grid=(depth, n // tn),
        in_specs=[pl.BlockSpec((nb, d, LANES), lambda l, j: (0, 0, 0)),
                  pl.BlockSpec((1, d, tn), lambda l, j: (l, 0, j)),
                  pl.BlockSpec((1, 1, tn), lambda l, j: (l, 0, j))],
        out_specs=pl.BlockSpec((1, nb, tn), lambda l, j: (l, 0, j)),
        out_shape=jax.ShapeDtypeStruct((depth, nb, n), F32),
        compiler_params=_cparams(2),
        name="adaln_mod",
    )(c_b, ada_w, ada_b.reshape(depth, 1, n))


def _from_token_major(ref, rows, nch):
    return jnp.concatenate([ref[pl.ds(c, rows, stride=nch), :] for c in range(nch)], axis=1)


def _to_token_major(ref, val, nch):
    rows = val.shape[0]
    for c in range(nch):
        ref[pl.ds(c, rows, stride=nch), :] = val[:, c * LANES:(c + 1) * LANES]


def _norm_kernel(*refs, n_add, tok_gate, write_x, side, h_dtype, add_tm, h_tm):
    it = iter(refs)
    x_ref = next(it)
    add_refs = [next(it) for _ in range(n_add)]
    tg_refs = [next(it) for _ in range(n_add)] if tok_gate else []
    gv_ref = next(it) if n_add else None
    g_ref, sc_ref, sh_ref = next(it), next(it), next(it)
    if side:
        ws_hi_ref, ws_lo_ref, bs_ref = next(it), next(it), next(it)
    xo_ref = next(it) if write_x else None
    h_ref = next(it)
    s_ref = next(it) if side else None

    x = x_ref[...]
    tm, d = x.shape
    nch = d // LANES
    if n_add:
        load = (lambda a: _from_token_major(a, tm, nch)) if add_tm else (lambda a: a[...].astype(F32))
        if tok_gate:
            upd = sum(load(a) * tg[:, 0:1] for a, tg in zip(add_refs, tg_refs))
        else:
            upd = sum(load(a) for a in add_refs)
        x = x + gv_ref[0] * upd
    if write_x:
        xo_ref[...] = x
    y = x * lax.rsqrt(jnp.mean(x * x, axis=-1, keepdims=True) + EPS)
    h = (y * g_ref[...]) * (1.0 + sc_ref[0]) + sh_ref[0]
    if h_tm:
        _to_token_major(h_ref, h.astype(h_dtype), nch)
    else:
        h_ref[...] = h.astype(h_dtype)
    if side:
        h_hi = h.astype(BF16)
        s = _dot(h_hi, ws_hi_ref[...])
        if side == "x3":
            h_lo = (h - h_hi.astype(F32)).astype(BF16)
            s = s + _dot(h_hi, ws_lo_ref[...]) + _dot(h_lo, ws_hi_ref[...])
        s_ref[...] = s + bs_ref[...]


def _norm_call(x, adds, tok_gates, gate_vec, g, sc, sh, side_w, side_b, *, seq, write_x, side,
               h_dtype, add_row_offsets=None, add_tm=False, h_tm=False):
    t, d = x.shape
    tm = 256
    nch = d // LANES
    nsb = seq // tm
    n_add = len(adds)
    tok_gate = bool(tok_gates)
    row = lambda i: (i, 0)
    vec = lambda i: (i // nsb, 0, 0)
    args, specs = [x], [pl.BlockSpec((tm, d), row)]
    for k, a in enumerate(adds):
        off = 0 if add_row_offsets is None else add_row_offsets[k] // tm
        args.append(a)
        blk = (tm * nch, LANES) if add_tm else (tm, d)
        specs.append(pl.BlockSpec(blk, lambda i, off=off: (i + off, 0)))
    for tg in tok_gates:
        args.append(tg)
        specs.append(pl.BlockSpec((tm, LANES), row))
    if n_add:
        args.append(gate_vec)
        specs.append(pl.BlockSpec((1, 1, d), vec))
    args += [g.reshape(1, d), sc, sh]
    specs += [pl.BlockSpec((1, d), lambda i: (0, 0)), pl.BlockSpec((1, 1, d), vec),
              pl.BlockSpec((1, 1, d), vec)]
    if side:
        w_hi = side_w.astype(BF16)
        w_lo = (side_w - w_hi.astype(F32)).astype(BF16)
        args += [w_hi, w_lo, side_b]
        specs += [pl.BlockSpec((d, LANES), lambda i: (0, 0)), pl.BlockSpec((d, LANES), lambda i: (0, 0)),
                  pl.BlockSpec((1, LANES), lambda i: (0, 0))]
    out_shape, out_specs = [], []
    if write_x:
        out_shape.append(jax.ShapeDtypeStruct((t, d), F32))
        out_specs.append(pl.BlockSpec((tm, d), row))
    if h_tm:
        out_shape.append(jax.ShapeDtypeStruct((t * nch, LANES), h_dtype))
        out_specs.append(pl.BlockSpec((tm * nch, LANES), row))
    else:
        out_shape.append(jax.ShapeDtypeStruct((t, d), h_dtype))
        out_specs.append(pl.BlockSpec((tm, d), row))
    if side:
        out_shape.append(jax.ShapeDtypeStruct((t, LANES), F32))
        out_specs.append(pl.BlockSpec((tm, LANES), row))
    return pl.pallas_call(
        functools.partial(_norm_kernel, n_add=n_add, tok_gate=tok_gate, write_x=write_x, side=side,
                          h_dtype=h_dtype, add_tm=add_tm, h_tm=h_tm),
        grid=(t // tm,), in_specs=specs, out_specs=out_specs, out_shape=out_shape,
        compiler_params=_cparams(1), name="resid_norm",
    )(*args)


def _mm_kernel(a_ref, w_ref, o_ref):
    o_ref[...] = _dot(a_ref[...], w_ref[0]).astype(o_ref.dtype)


def _matmul(a, w_stack, layer, out_dtype, tm=1024, tn=512):
    m, k = a.shape
    n = w_stack.shape[2]
    tm = min(tm, m)
    return pl.pallas_call(
        _mm_kernel, grid=(m // tm, n // tn),
        in_specs=[pl.BlockSpec((tm, k), lambda i, j: (i, 0)),
                  pl.BlockSpec((1, k, tn), lambda i, j: (layer, 0, j))],
        out_specs=pl.BlockSpec((tm, tn), lambda i, j: (i, j)),
        out_shape=jax.ShapeDtypeStruct((m, n), out_dtype),
        compiler_params=_cparams(2), name="in_proj",
    )(a, w_stack)


def _mm3_kernel(a1_ref, a2_ref, a3_ref, w_ref, o_ref):
    k1, k2 = a1_ref.shape[1], a2_ref.shape[1]
    acc = _dot(a1_ref[...], w_ref[0, 0:k1, :])
    acc = acc + _dot(a2_ref[...], w_ref[0, k1:k1 + k2, :])
    acc = acc + _dot(a3_ref[...], w_ref[0, k1 + k2:, :])
    o_ref[...] = acc.astype(o_ref.dtype)


def _out_proj(y_ret, y_m, y_f, w_stack, layer, out_dtype, tm=1024, tn=512):
    m = y_ret.shape[0]
    _, k, n = w_stack.shape
    tm = min(tm, m)
    return pl.pallas_call(
        _mm3_kernel, grid=(m // tm, n // tn),
        in_specs=[pl.BlockSpec((tm, y_ret.shape[1]), lambda i, j: (i, 0)),
                  pl.BlockSpec((tm, y_m.shape[1]), lambda i, j: (i, 0)),
                  pl.BlockSpec((tm, y_f.shape[1]), lambda i, j: (i, 0)),
                  pl.BlockSpec((1, k, tn), lambda i, j: (layer, 0, j))],
        out_specs=pl.BlockSpec((tm, tn), lambda i, j: (i, j)),
        out_shape=jax.ShapeDtypeStruct((m, n), out_dtype),
        compiler_params=_cparams(2), name="out_proj",
    )(y_ret, y_m, y_f, w_stack)


def _ret_kernel(q_ref, k_ref, v_ref, g_ref, cos_ref, sin_ref, decay_ref, zeta_ref, xi_ref, gam_ref,
                gn_ref, o_ref, state_ref, *, ts):
    @pl.when(pl.program_id(1) == 0)
    def _():
        state_ref[...] = jnp.zeros_like(state_ref)

    for c in range(ts // CHUNK):
        rows = slice(c * CHUNK, (c + 1) * CHUNK)
        cosv, sinv = cos_ref[rows, :], sin_ref[rows, :]
        for h in range(RET_HEADS):
            cols = slice(h * HEAD_DIM, (h + 1) * HEAD_DIM)
            q = q_ref[rows, cols].astype(F32)
            k = k_ref[rows, cols].astype(F32)
            qr = q * cosv + pltpu.roll(q, HEAD_DIM // 2, 1) * sinv
            kr = k * cosv + pltpu.roll(k, HEAD_DIM // 2, 1) * sinv
            qb, kb, vb = qr.astype(BF16), kr.astype(BF16), v_ref[rows, cols]
            s = _dot_nt(qb, kb) * decay_ref[h]
            state = state_ref[h]
            y = _dot(s.astype(BF16), vb) + _dot(qb, state.astype(BF16)) * xi_ref[h]
            kz_t = (kr * zeta_ref[h]).T.astype(BF16)
            state_ref[h] = state * gam_ref[h] + _dot(kz_t, vb)
            out = _group_norm(y, gn_ref[:, cols]) * _silu(g_ref[rows, cols].astype(F32))
            o_ref[rows, cols] = out.astype(o_ref.dtype)


def _retention(proj, gn_g, consts, *, batch, seq):
    ts = 512
    ns = seq // ts
    cosf, sins, decay, zeta, xi, gam = consts
    rowblk = lambda cb: pl.BlockSpec((ts, D_RET), lambda b, s, cb=cb: (b * ns + s, cb))
    full3 = pl.BlockSpec((RET_HEADS, CHUNK, HEAD_DIM), lambda b, s: (0, 0, 0))
    return pl.pallas_call(
        functools.partial(_ret_kernel, ts=ts), grid=(batch, ns),
        in_specs=[rowblk(COL_RQ // D_RET), rowblk(COL_RK // D_RET), rowblk(COL_RV // D_RET),
                  rowblk(COL_RG // D_RET),
                  pl.BlockSpec((ts, HEAD_DIM), lambda b, s: (s, 0)),
                  pl.BlockSpec((ts, HEAD_DIM), lambda b, s: (s, 0)),
                  full3, full3, full3,
                  pl.BlockSpec((RET_HEADS, 1, HEAD_DIM), lambda b, s: (0, 0, 0)),
                  pl.BlockSpec((1, D_RET), lambda b, s: (0, 0))],
        out_specs=pl.BlockSpec((ts, D_RET), lambda b, s: (b * ns + s, 0)),
        out_shape=jax.ShapeDtypeStruct((batch * seq, D_RET), BF16),
        scratch_shapes=[pltpu.VMEM((RET_HEADS, HEAD_DIM, HEAD_DIM), F32)],
        compiler_params=_cparams(2), name="retention",
    )(proj, proj, proj, proj, cosf, sins, decay, zeta, xi, gam, gn_g.reshape(1, D_RET))


def _retention_consts(seq):
    h = RET_HEADS
    log_gamma = jnp.log1p(-jnp.power(2.0, -5.0 - jnp.arange(h, dtype=F32)))
    pos = jnp.arange(CHUNK, dtype=F32)
    diff = pos[:, None] - pos[None, :]
    decay = jnp.where(diff >= 0, jnp.exp(jnp.maximum(diff, 0.0)[None] * log_gamma[:, None, None]), 0.0)
    zeta = jnp.exp((CHUNK - 1 - pos)[None, :] * log_gamma[:, None])
    xi = jnp.exp((pos + 1)[None, :] * log_gamma[:, None])
    gam = jnp.exp(CHUNK * log_gamma)
    bcast = lambda v: jnp.broadcast_to(v[:, :, None], (h, CHUNK, HEAD_DIM))
    half = HEAD_DIM // 2
    inv = ROPE_BASE ** (-jnp.arange(half, dtype=F32) / half)
    ang = jnp.arange(seq, dtype=F32)[:, None] * inv[None, :]
    cosf = jnp.concatenate([jnp.cos(ang), jnp.cos(ang)], axis=1)
    sins = jnp.concatenate([-jnp.sin(ang), jnp.sin(ang)], axis=1)
    gam_b = jnp.broadcast_to(gam[:, None, None], (h, 1, HEAD_DIM))
    return cosf, sins, decay, bcast(zeta), bcast(xi), gam_b


def _mlstm_kernel(x_ref, v_ref, og_ref, gates_ref, cw_ref, cb_ref, wq_ref, wk_ref, gn_ref, o_ref,
                  xbuf, c_state, n_state, m_state, *, ts):
    @pl.when(pl.program_id(1) == 0)
    def _():
        c_state[...] = jnp.zeros_like(c_state)
        n_state[...] = jnp.zeros_like(n_state)
        m_state[...] = jnp.full_like(m_state, NEG_INF)
        xbuf[0:8, :] = jnp.zeros((8, D_MLSTM), F32)

    x = x_ref[...].astype(F32)
    xbuf[8:ts + 8, :] = x
    conv = cb_ref[...]
    for j in range(CONV_WIDTH):
        start = 8 - (CONV_WIDTH - 1) + j
        conv = conv + cw_ref[j:j + 1, :] * xbuf[start:start + ts, :]
    xbuf[0:8, :] = x[ts - 8:ts, :]
    xc = _silu(conv).astype(BF16)

    tri = _lower_tri(CHUNK, BF16)
    r_i = lax.broadcasted_iota(jnp.int32, (CHUNK, CHUNK), 0)
    c_i = lax.broadcasted_iota(jnp.int32, (CHUNK, CHUNK), 1)
    causal = c_i <= r_i

    q_all = [_dot(xc[:, h * HEAD_DIM:(h + 1) * HEAD_DIM], wq_ref[h]).astype(BF16) for h in range(MLSTM_HEADS)]
    k_all = [_dot(xc[:, h * HEAD_DIM:(h + 1) * HEAD_DIM], wk_ref[h]) for h in range(MLSTM_HEADS)]

    lane8 = lax.broadcasted_iota(jnp.int32, (SUBLANES, CHUNK), 1)
    ones_cols = jnp.ones((CHUNK, HEAD_DIM), BF16)
    square = (CHUNK, CHUNK)

    def per_token_cols(row):
        return jnp.broadcast_to(row, square).T

    for c in range(ts // CHUNK):
        rows = slice(c * CHUNK, (c + 1) * CHUNK)
        gc = gates_ref[rows, :]
        b_col = _cumsum_rows(tri, _log_sigmoid(gc))
        li8 = gc.T[LANE_MI:LANE_MI + SUBLANES, :]
        b8 = pltpu.roll(b_col.T[0:SUBLANES, :], SUBLANES - LANE_MF, 0)
        g8 = li8 - b8
        cm8 = g8
        shift = 1
        while shift < CHUNK:
            cm8 = jnp.maximum(cm8, jnp.where(lane8 >= shift, pltpu.roll(cm8, shift, 1), NEG_INF))
            shift *= 2
        m_prev8 = m_state[...]
        big_m8 = jnp.maximum(cm8, m_prev8)
        m_t8 = b8 + big_m8
        b_end8 = jnp.broadcast_to(b8[:, CHUNK - 1:CHUNK], (SUBLANES, CHUNK))
        a8 = b_end8 + g8
        a_max8 = jnp.broadcast_to(jnp.max(a8, axis=1, keepdims=True), (SUBLANES, CHUNK))
        w8 = jnp.exp(a8 - a_max8)
        m_new8 = jnp.maximum(b_end8 + m_prev8, a_max8)
        s_old8 = jnp.exp(b_end8 + m_prev8 - m_new8)
        s_new8 = jnp.exp(a_max8 - m_new8)
        m_state[...] = m_new8

        for h in range(MLSTM_HEADS):
            cols = slice(h * HEAD_DIM, (h + 1) * HEAD_DIM)
            hrow = slice(h, h + 1)
            c_prev, n_prev = c_state[h], n_state[h]
            big_m = per_token_cols(big_m8[hrow, :])
            d_w = jnp.exp(jnp.where(causal, jnp.broadcast_to(g8[hrow, :], square) - big_m, NEG_INF))
            inter_w = jnp.exp(m_prev8[hrow, :] - big_m)
            floor = jnp.exp(-per_token_cols(m_t8[hrow, :]))

            q = q_all[h][rows, :]
            kf = k_all[h][rows, :]
            v_ones = jnp.concatenate([v_ref[rows, cols], ones_cols], axis=1)
            s = _dot_nt(q, kf.astype(BF16)) * d_w
            sv = _dot(s.astype(BF16), v_ones)
            qc = _dot(q, jnp.concatenate([c_prev, n_prev], axis=1).astype(BF16))
            num = sv[:, :HEAD_DIM] + inter_w * qc[:, :HEAD_DIM]
            den = sv[:, HEAD_DIM:] + inter_w * qc[:, HEAD_DIM:]
            hh = num / jnp.maximum(jnp.abs(den), floor)

            kw_t = (kf.T * w8[hrow, :]).astype(BF16)
            kv = _dot(kw_t, v_ones)
            c_state[h] = s_old8[hrow, :] * c_prev + s_new8[hrow, :] * kv[:, :HEAD_DIM]
            n_state[h] = s_old8[hrow, :] * n_prev + s_new8[hrow, :] * kv[:, HEAD_DIM:]

            out = jax.nn.sigmoid(og_ref[rows, cols].astype(F32)) * _group_norm(hh, gn_ref[:, cols])
            o_ref[rows, cols] = out.astype(o_ref.dtype)


def _mlstm(proj, gates, conv_w, conv_b, wq, wk, gn_g, *, batch, seq):
    ts = 512
    ns = seq // ts
    rowblk = lambda cb: pl.BlockSpec((ts, D_MLSTM), lambda b, s, cb=cb: (b * ns + s, cb))
    wspec = pl.BlockSpec((MLSTM_HEADS, HEAD_DIM, HEAD_DIM), lambda b, s: (0, 0, 0))
    return pl.pallas_call(
        functools.partial(_mlstm_kernel, ts=ts), grid=(batch, ns),
        in_specs=[rowblk(COL_MX // D_MLSTM), rowblk(COL_MV // D_MLSTM), rowblk(COL_MO // D_MLSTM),
                  pl.BlockSpec((ts, LANES), lambda b, s: (b * ns + s, 0)),
                  pl.BlockSpec((CONV_WIDTH, D_MLSTM), lambda b, s: (0, 0)),
                  pl.BlockSpec((1, D_MLSTM), lambda b, s: (0, 0)),
                  wspec, wspec,
                  pl.BlockSpec((1, D_MLSTM), lambda b, s: (0, 0))],
        out_specs=pl.BlockSpec((ts, D_MLSTM), lambda b, s: (b * ns + s, 0)),
        out_shape=jax.ShapeDtypeStruct((batch * seq, D_MLSTM), BF16),
        scratch_shapes=[pltpu.VMEM((ts + 8, D_MLSTM), F32),
                        pltpu.VMEM((MLSTM_HEADS, HEAD_DIM, HEAD_DIM), F32),
                        pltpu.VMEM((MLSTM_HEADS, HEAD_DIM, HEAD_DIM), F32),
                        pltpu.VMEM((SUBLANES, CHUNK), F32)],
        compiler_params=_cparams(2), name="mlstm",
    )(proj, proj, proj, gates, conv_w, conv_b.reshape(1, D_MLSTM), wq.astype(BF16),
      (wk * HEAD_DIM ** -0.5).astype(BF16), gn_g.reshape(1, D_MLSTM))


def _fcum_kernel(gates_ref, fcol_ref, frow_ref, carry_ref, *, ts):
    @pl.when(pl.program_id(1) == 0)
    def _():
        carry_ref[...] = jnp.zeros_like(carry_ref)

    tri = _lower_tri(CHUNK, BF16)
    carry = carry_ref[...]
    for c in range(ts // CHUNK):
        rows = slice(c * CHUNK, (c + 1) * CHUNK)
        cum = _cumsum_rows(tri, _log_sigmoid(gates_ref[rows, :])) + carry
        carry = cum[CHUNK - 1:CHUNK, :]
        fcol_ref[rows, :] = cum
        frow_ref[0, :, rows] = cum.T[LANE_FF:LANE_FF + FOX_HEADS, :]
    carry_ref[...] = carry


def _forget_cumsum(gates, *, batch, seq):
    ts = 512
    ns = seq // ts
    return pl.pallas_call(
        functools.partial(_fcum_kernel, ts=ts), grid=(batch, ns),
        in_specs=[pl.BlockSpec((ts, LANES), lambda b, s: (b * ns + s, 0))],
        out_specs=[pl.BlockSpec((ts, LANES), lambda b, s: (b * ns + s, 0)),
                   pl.BlockSpec((1, FOX_HEADS, ts), lambda b, s: (b, 0, s))],
        out_shape=[jax.ShapeDtypeStruct((batch * seq, LANES), F32),
                   jax.ShapeDtypeStruct((batch, FOX_HEADS, seq), F32)],
        scratch_shapes=[pltpu.VMEM((1, LANES), F32)],
        compiler_params=_cparams(2), name="forget_cumsum",
    )(gates)


def _fox_kernel(q_ref, k_ref, v_ref, fcol_ref, frow_ref, o_ref, s_ref, acc_ref, m_ref, l_ref, *, tq):
    h = pl.program_id(1)
    i = pl.program_id(2)
    q = q_ref[...]
    tk = 2 * tq
    lane = lax.broadcasted_iota(jnp.int32, (tq, LANES), 1)
    f_q = jnp.sum(jnp.where(lane == LANE_FF + h, fcol_ref[...], 0.0), axis=1, keepdims=True)
    f_q = jnp.broadcast_to(f_q, (tq, LANES))
    lane_tiles = lambda a: [a[:, c * LANES:(c + 1) * LANES] for c in range(a.shape[1] // LANES)]

    def scores(start, width):
        return _dot_nt(q, k_ref[pl.ds(start, width), :]) - frow_ref[0, :, pl.ds(start, width)]

    def keep(start, s):
        s_ref[:, pl.ds(start, s.shape[1])] = s
        m = m_ref[...]
        for s_c in lane_tiles(s):
            m = jnp.maximum(m, s_c)
        m_ref[...] = m

    def weigh(start, width):
        ps = [jnp.exp(s_c + shift) for s_c in lane_tiles(s_ref[:, pl.ds(start, width)])]
        l_ref[...] = l_ref[...] + sum(ps[1:], ps[0])
        p = jnp.concatenate(ps, axis=1).astype(BF16)
        acc_ref[...] = acc_ref[...] + _dot(p, v_ref[pl.ds(start, width), :])

    def sweep(n_keys, fn):
        n_wide = n_keys // tk

        def body(j, carry):
            fn(pl.multiple_of(j * tk, tk), tk)
            return carry

        lax.fori_loop(0, n_wide, body, 0)

        @pl.when(n_wide * tk < n_keys)
        def _():
            fn(pl.multiple_of(n_wide * tk, tq), tq)

    m_ref[...] = jnp.full_like(m_ref, NEG_INF)
    diag = pl.multiple_of(i * tq, tq)
    sweep(diag, lambda start, width: keep(start, scores(start, width)))
    r_i = lax.broadcasted_iota(jnp.int32, (tq, tq), 0)
    c_i = lax.broadcasted_iota(jnp.int32, (tq, tq), 1)
    keep(diag, jnp.where(c_i <= r_i, scores(diag, tq), NEG_INF))

    m_row = jnp.max(m_ref[...], axis=1, keepdims=True) + f_q
    shift = f_q - m_row
    acc_ref[...] = jnp.zeros_like(acc_ref)
    l_ref[...] = jnp.zeros_like(l_ref)
    sweep(diag + tq, weigh)
    l_row = jnp.sum(l_ref[...], axis=1, keepdims=True)
    o_ref[...] = (acc_ref[...] / l_row).astype(o_ref.dtype)


def _fox_attention(proj, fcol, frow, *, batch, seq):
    tq = 512
    nq = seq // tq
    cq, ck, cv = COL_FQ // HEAD_DIM, COL_FK // HEAD_DIM, COL_FV // HEAD_DIM
    return pl.pallas_call(
        functools.partial(_fox_kernel, tq=tq), grid=(batch, FOX_HEADS, nq),
        in_specs=[pl.BlockSpec((tq, HEAD_DIM), lambda b, h, i: (b * nq + i, cq + h)),
                  pl.BlockSpec((seq, HEAD_DIM), lambda b, h, i: (b, ck + h)),
                  pl.BlockSpec((seq, HEAD_DIM), lambda b, h, i: (b, cv + h)),
                  pl.BlockSpec((tq, LANES), lambda b, h, i: (b * nq + i, 0)),
                  pl.BlockSpec((1, 1, seq), lambda b, h, i: (b * FOX_HEADS + h, 0, 0))],
        out_specs=pl.BlockSpec((tq, HEAD_DIM), lambda b, h, i: (b * nq + i, h)),
        out_shape=jax.ShapeDtypeStruct((batch * seq, D_FOX), BF16),
        scratch_shapes=[pltpu.VMEM((tq, seq), F32), pltpu.VMEM((tq, HEAD_DIM), F32),
                        pltpu.VMEM((tq, LANES), F32), pltpu.VMEM((tq, LANES), F32)],
        compiler_params=_cparams(3), name="fox_attention",
    )(proj, proj, proj, fcol, frow.reshape(batch * FOX_HEADS, 1, seq))


def _route_kernel(logit_ref, eid_ref, gate_ref, rank_ref, cnt_ref, upper_ref, base_ref, *, tm):
    @pl.when(pl.program_id(0) == 0)
    def _():
        r = lax.broadcasted_iota(jnp.int32, (tm, tm), 0)
        c = lax.broadcasted_iota(jnp.int32, (tm, tm), 1)
        upper_ref[...] = jnp.where(r < c, 1.0, 0.0).astype(BF16)
        base_ref[...] = jnp.zeros_like(base_ref)

    lt = logit_ref[...].T
    e = lt[LANE_RE:LANE_RE + N_EXPERTS, :]
    g = lt[LANE_RG:LANE_RG + N_GROUPS, :]
    g_max = jnp.max(g, axis=0, keepdims=True)
    g_val = 1.0 / jnp.sum(jnp.exp(g - g_max), axis=0, keepdims=True)
    g_row = lax.broadcasted_iota(jnp.int32, (N_GROUPS, tm), 0).astype(F32)
    g_idx = jnp.min(jnp.where(g == g_max, g_row, float(N_GROUPS)), axis=0, keepdims=True)

    e_row_i = lax.broadcasted_iota(jnp.int32, (N_EXPERTS, tm), 0)
    e_row = e_row_i.astype(F32)
    e_grp = jnp.right_shift(e_row_i, EXPERTS_PER_GROUP.bit_length() - 1).astype(F32)
    el = jnp.where(e_grp == g_idx, e, NEG_INF)
    max1 = jnp.max(el, axis=0, keepdims=True)
    idx1 = jnp.min(jnp.where(el == max1, e_row, float(N_EXPERTS)), axis=0, keepdims=True)
    el2 = jnp.where(e_row == idx1, NEG_INF, el)
    max2 = jnp.max(el2, axis=0, keepdims=True)
    idx2 = jnp.min(jnp.where(el2 == max2, e_row, float(N_EXPERTS)), axis=0, keepdims=True)
    e_sum = jnp.sum(jnp.exp(el - max1), axis=0, keepdims=True)
    p1 = 1.0 / e_sum
    p2 = jnp.exp(max2 - max1) / e_sum
    p_sum = p1 + p2
    gate_ref[0:1, :] = g_val * (p1 / p_sum)
    gate_ref[1:2, :] = g_val * (p2 / p_sum)
    eid_ref[0:1, :] = idx1.astype(jnp.int32)
    eid_ref[1:2, :] = idx2.astype(jnp.int32)

    oh1 = e_row == idx1
    oh2 = e_row == idx2
    onehot = jnp.where(oh1 | oh2, 1.0, 0.0)
    before = _dot(onehot.astype(BF16), upper_ref[...]) + base_ref[:, 0:1]
    rank_ref[0:1, :] = jnp.sum(jnp.where(oh1, before, 0.0), axis=0, keepdims=True).astype(jnp.int32)
    rank_ref[1:2, :] = jnp.sum(jnp.where(oh2, before, 0.0), axis=0, keepdims=True).astype(jnp.int32)
    base_ref[...] = base_ref[...] + jnp.sum(onehot, axis=1, keepdims=True)
    cnt_ref[...] = base_ref[...]


def _route(logits):
    t = logits.shape[0]
    tm = min(1024, t)
    tok = lambda i: (0, i)
    return pl.pallas_call(
        functools.partial(_route_kernel, tm=tm), grid=(t // tm,),
        in_specs=[pl.BlockSpec((tm, LANES), lambda i: (i, 0))],
        out_specs=[pl.BlockSpec((2, tm), tok), pl.BlockSpec((2, tm), tok), pl.BlockSpec((2, tm), tok),
                   pl.BlockSpec((N_EXPERTS, LANES), lambda i: (0, 0))],
        out_shape=[jax.ShapeDtypeStruct((2, t), jnp.int32), jax.ShapeDtypeStruct((2, t), F32),
                   jax.ShapeDtypeStruct((2, t), jnp.int32),
                   jax.ShapeDtypeStruct((N_EXPERTS, LANES), F32)],
        scratch_shapes=[pltpu.VMEM((tm, tm), BF16), pltpu.VMEM((N_EXPERTS, LANES), F32)],
        compiler_params=_cparams(1), name="route",
    )(logits)


def _moe_kernel(tile_expert, n_active, row_id, h_hbm, w1_ref, w3_ref, w2_ref, o_hbm,
                xbuf, xb, ybuf, gsem, ssem, *, tm, nch, n_tokens, n_tiles):
    del tile_expert
    i = pl.program_id(0)
    n_act = n_active[0]
    last_step = jnp.minimum(n_act, n_tiles - 1)

    def token_of(rid):
        if n_tokens & (n_tokens - 1) == 0:
            return jnp.bitwise_and(rid, n_tokens - 1)
        return rid - n_tokens * ((rid >= n_tokens).astype(jnp.int32) + (rid >= 2 * n_tokens).astype(jnp.int32))

    def hbm_token(ref, tok):
        return ref.at[pl.ds(pl.multiple_of(tok * nch, nch), nch)]

    def vmem_row(ref, r):
        if isinstance(r, int):
            return ref.at[r // SUBLANES, :, r % SUBLANES, :]
        return ref.at[r // SUBLANES, :, lax.rem(r, SUBLANES), :]

    def start_gather(tile, r):
        tok = token_of(row_id[(tile + 1) * tm + r])
        pltpu.make_async_copy(hbm_token(h_hbm, tok), vmem_row(xbuf, r), gsem).start()

    def start_scatter(tile, r):
        dst = row_id[(tile + 1) * tm + r]
        pltpu.make_async_copy(vmem_row(ybuf, r), hbm_token(o_hbm, dst), ssem).start(priority=1)

    def wait_rows(sem):
        for _ in range(tm):
            pltpu.make_async_copy(hbm_token(h_hbm, 0), vmem_row(xbuf, 0), sem).wait()

    def rows_loop(fn):
        def body(r, carry):
            fn(r)
            return carry
        lax.fori_loop(0, tm, body, 0, unroll=8)

    @pl.when(i <= last_step)
    def _():
        @pl.when(i == 0)
        def _():
            ybuf[...] = jnp.zeros_like(ybuf)
            rows_loop(lambda r: start_gather(0, r))

        wait_rows(gsem)
        xb[...] = jnp.concatenate([xbuf[:, c].reshape(tm, LANES) for c in range(nch)], axis=1).astype(BF16)
        nxt = jnp.minimum(i + 1, n_tiles - 1)
        for r in range(tm):
            start_gather(nxt, r)
            start_scatter(i - 1, r)

        @pl.when(n_act >= i)
        def _():
            x = xb[...]
            h1 = _dot(x, w1_ref[0, 0].astype(BF16))
            h3 = _dot(x, w3_ref[0, 0].astype(BF16))
            act = (_silu(h1) * h3).astype(BF16)
            y = _dot(act, w2_ref[0, 0].astype(BF16))
            wait_rows(ssem)
            for c in range(nch):
                ybuf[:, c] = y[:, c * LANES:(c + 1) * LANES].reshape(tm // SUBLANES, SUBLANES, LANES)

        @pl.when(i == last_step)
        def _():
            wait_rows(gsem)

            @pl.when(i < n_act)
            def _():
                rows_loop(lambda r: start_scatter(i, r))
                wait_rows(ssem)


def _moe_experts(h2_tm, w1, w3, w2, layer, tile_expert, n_active, row_id, *, n_tiles):
    _, _, d, de = w1.shape
    nch = d // LANES
    t = h2_tm.shape[0] // nch
    tm = MOE_TILE
    wmap = lambda i, te, na, rid: (layer, te[i], 0, 0)
    grid_spec = pltpu.PrefetchScalarGridSpec(
        num_scalar_prefetch=3, grid=(n_tiles,),
        in_specs=[pl.BlockSpec(memory_space=pl.ANY),
                  pl.BlockSpec((1, 1, d, de), wmap), pl.BlockSpec((1, 1, d, de), wmap),
                  pl.BlockSpec((1, 1, de, d), wmap)],
        out_specs=pl.BlockSpec(memory_space=pl.ANY),
        scratch_shapes=[pltpu.VMEM((tm // SUBLANES, nch, SUBLANES, LANES), F32), pltpu.VMEM((tm, d), BF16),
                        pltpu.VMEM((tm // SUBLANES, nch, SUBLANES, LANES), F32),
                        pltpu.SemaphoreType.DMA(()), pltpu.SemaphoreType.DMA(())])
    return pl.pallas_call(
        functools.partial(_moe_kernel, tm=tm, nch=nch, n_tokens=t, n_tiles=n_tiles), grid_spec=grid_spec,
        out_shape=jax.ShapeDtypeStruct(((2 * t + tm) * nch, LANES), F32),
        compiler_params=_cparams(1), name="moe_experts",
    )(tile_expert, n_active, row_id, h2_tm, w1, w3, w2)


def _dispatch_plan(eid, rank, counts, *, n_tokens, n_tiles):
    tm = MOE_TILE
    cnt = counts[:, 0].astype(jnp.int32)
    tiles_e = (cnt + tm - 1) // tm
    tile_end = jnp.cumsum(tiles_e)
    tile_start = tile_end - tiles_e
    experts = jnp.arange(N_EXPERTS, dtype=jnp.int32)
    row_off = jnp.sum(jnp.where(eid[..., None] == experts, tile_start * tm, 0), axis=-1)
    dest = row_off + rank
    n_active = tile_end[-1]
    tile_ids = jnp.arange(n_tiles, dtype=jnp.int32)
    last_tile = jnp.minimum(tile_ids, n_active - 1)
    tile_expert = jnp.sum((tile_end[None, :] <= last_tile[:, None]).astype(jnp.int32), axis=-1)
    tile_expert = jnp.minimum(tile_expert, N_EXPERTS - 1)
    slot_row = (jnp.arange(n_tokens, dtype=jnp.int32)[None, :]
                + jnp.arange(2, dtype=jnp.int32)[:, None] * n_tokens)
    spare = 2 * n_tokens + jnp.arange((n_tiles + 1) * tm, dtype=jnp.int32) % tm
    row_id = spare.at[tm + dest.reshape(-1)].set(slot_row.reshape(-1))
    return tile_expert, n_active.reshape(1), row_id


W_IN_GATES0 = COL_MO + D_MLSTM
W_IN_FOX0 = W_IN_GATES0 + 2 * MLSTM_HEADS
W_IN_GATES1 = W_IN_FOX0 + 3 * D_FOX


def _wprep_kernel(w_ref, o_ref, g_ref):
    s = HEAD_DIM ** -0.5
    w = w_ref[0]

    def put(dst, src):
        o_ref[0, :, dst:dst + src.shape[1]] = src.astype(BF16)

    put(COL_RQ, w[:, 0:COL_RK])
    put(COL_RK, w[:, COL_RK:COL_RV] * s)
    put(COL_RV, w[:, COL_RV:W_IN_GATES0])
    put(COL_FQ, w[:, W_IN_FOX0:W_IN_FOX0 + D_FOX] * s)
    put(COL_FK, w[:, W_IN_FOX0 + D_FOX:W_IN_GATES1])
    n_gate = 2 * MLSTM_HEADS + FOX_HEADS
    g_ref[0] = jnp.concatenate([w[:, W_IN_GATES0:W_IN_FOX0], w[:, W_IN_GATES1:W_IN_GATES1 + FOX_HEADS],
                                jnp.zeros((w.shape[0], LANES - n_gate), F32)], axis=1)


def _rearranged_w_in(w_in):
    depth, d, n_in = w_in.shape
    tr = 256
    return pl.pallas_call(
        _wprep_kernel, grid=(depth, d // tr),
        in_specs=[pl.BlockSpec((1, tr, n_in), lambda l, r: (l, r, 0))],
        out_specs=[pl.BlockSpec((1, tr, N_PROJ), lambda l, r: (l, r, 0)),
                   pl.BlockSpec((1, tr, LANES), lambda l, r: (l, r, 0))],
        out_shape=[jax.ShapeDtypeStruct((depth, d, N_PROJ), BF16),
                   jax.ShapeDtypeStruct((depth, d, LANES), F32)],
        compiler_params=_cparams(2), name="w_in_prep",
    )(w_in)


def _lane_pad(v):
    return jnp.pad(v, ((0, 0), (0, LANES - v.shape[-1])))


def kernel(x, c, ada_w, ada_b, norm1_g, w_in, ret_gn_g, mlstm_conv_w, mlstm_conv_b, mlstm_wq, mlstm_wk,
           mlstm_i_b, mlstm_f_b, mlstm_gn_g, fox_f_b, w_out, norm2_g, router_group_w, router_group_b,
           router_expert_w, router_expert_b, moe_w1, moe_w3, moe_w2, final_g):
    batch, seq, d = x.shape
    depth = ada_w.shape[0]
    t = batch * seq
    n_tiles = (2 * t) // MOE_TILE + N_EXPERTS

    mod = _modulation(c, ada_w, ada_b)
    mod = mod.reshape(depth, batch, N_MOD, 1, d)
    w_big, w_gate = _rearranged_w_in(w_in)
    gate_bias = _lane_pad(jnp.concatenate([mlstm_i_b, mlstm_f_b, fox_f_b], axis=-1))
    w_route = jnp.concatenate([router_expert_w, router_group_w], axis=-1)
    w_route = jnp.pad(w_route, ((0, 0), (0, 0), (0, LANES - w_route.shape[-1])))
    route_bias = _lane_pad(jnp.concatenate([router_expert_b, router_group_b], axis=-1))
    w_out_b = w_out.astype(BF16)
    ret_consts = _retention_consts(seq)

    xf = x.reshape(t, d)
    moe_out, tok_gates, g2_prev = None, (), None
    for l in range(depth):
        sh1, sc1, g1, sh2, sc2, g2 = (mod[l, :, k] for k in range(N_MOD))
        if l == 0:
            h1, gates = _norm_call(xf, (), (), None, norm1_g[l], sc1, sh1, w_gate[l], gate_bias[l:l + 1],
                                   seq=seq, write_x=False, side="x1", h_dtype=BF16)
        else:
            xf, h1, gates = _norm_call(xf, (moe_out, moe_out), tok_gates, g2_prev, norm1_g[l], sc1, sh1,
                                       w_gate[l], gate_bias[l:l + 1], seq=seq, write_x=True, side="x1",
                                       h_dtype=BF16, add_row_offsets=(0, t), add_tm=True)
        proj = _matmul(h1, w_big, l, BF16)
        y_ret = _retention(proj, ret_gn_g[l], ret_consts, batch=batch, seq=seq)
        y_m = _mlstm(proj, gates, mlstm_conv_w[l], mlstm_conv_b[l], mlstm_wq[l], mlstm_wk[l],
                     mlstm_gn_g[l], batch=batch, seq=seq)
        fcol, frow = _forget_cumsum(gates, batch=batch, seq=seq)
        y_f = _fox_attention(proj, fcol, frow, batch=batch, seq=seq)
        mix = _out_proj(y_ret, y_m, y_f, w_out_b, l, BF16)
        xf, h2, logits = _norm_call(xf, (mix,), (), g1, norm2_g[l], sc2, sh2, w_route[l],
                                    route_bias[l:l + 1], seq=seq, write_x=True, side="x3", h_dtype=F32,
                                    h_tm=True)
        eid, gate, rank, counts = _route(logits)
        tile_expert, n_active, row_id = _dispatch_plan(eid, rank, counts, n_tokens=t, n_tiles=n_tiles)
        moe_out = _moe_experts(h2, moe_w1, moe_w3, moe_w2, l, tile_expert, n_active, row_id,
                               n_tiles=n_tiles)
        tok_gates = tuple(jnp.broadcast_to(gate[k][:, None], (t, LANES)) for k in range(2))
        g2_prev = g2
    zeros = jnp.zeros((batch, 1, d), F32)
    (out,) = _norm_call(xf, (moe_out, moe_out), tok_gates, g2_prev, final_g, zeros, zeros, None, None,
                        seq=seq, write_x=False, side=None, h_dtype=F32,
                        add_row_offsets=(0, t), add_tm=True)
    return out.reshape(batch, seq, d)
```

```python
import functools

import jax
import jax.numpy as jnp
from jax import lax
from jax.experimental import pallas as pl
from jax.experimental.pallas import tpu as pltpu

F32 = jnp.float32
BF16 = jnp.bfloat16

HEAD_DIM = 128
LANES = 128
SUBLANES = 8
RET_HEADS = 4
MLSTM_HEADS = 4
FOX_HEADS = 8
D_RET = RET_HEADS * HEAD_DIM
D_MLSTM = MLSTM_HEADS * HEAD_DIM
D_FOX = FOX_HEADS * HEAD_DIM
CHUNK = 128
CONV_WIDTH = 4
ROPE_BASE = 10000.0
N_GROUPS = 4
EXPERTS_PER_GROUP = 8
N_EXPERTS = N_GROUPS * EXPERTS_PER_GROUP
N_MOD = 6
EPS = 1e-6
NEG_INF = float("-inf")

COL_RQ, COL_RK, COL_RV, COL_RG = 0, 512, 1024, 1536
COL_MX, COL_MV, COL_MO = 2048, 2560, 3072
COL_FQ, COL_FK, COL_FV = 3584, 4608, 5632
N_PROJ = 6656
LANE_MI, LANE_MF, LANE_FF = 0, 4, 8
LANE_RE, LANE_RG = 0, 32

VMEM_LIMIT = 56 * 1024 * 1024
MOE_TILE = 512


def _cparams(n_grid):
    return pltpu.CompilerParams(dimension_semantics=("arbitrary",) * n_grid,
                                vmem_limit_bytes=VMEM_LIMIT)


def _dot(a, b):
    return jnp.dot(a, b, preferred_element_type=F32)


def _dot_nt(a, b):
    return lax.dot_general(a, b, (((1,), (1,)), ((), ())), preferred_element_type=F32)


def _silu(x):
    return x * jax.nn.sigmoid(x)


def _log_sigmoid(x):
    return jnp.minimum(x, 0.0) - jnp.log(1.0 + jnp.exp(-jnp.abs(x)))


def _split3(x):
    hi = x.astype(BF16)
    r1 = x - hi.astype(F32)
    mid = r1.astype(BF16)
    lo = (r1 - mid.astype(F32)).astype(BF16)
    return hi, mid, lo


def _cumsum_rows(tri, x):
    hi, mid, lo = _split3(x)
    return _dot(tri, hi) + _dot(tri, mid) + _dot(tri, lo)


def _lower_tri(n, dtype):
    r = lax.broadcasted_iota(jnp.int32, (n, n), 0)
    c = lax.broadcasted_iota(jnp.int32, (n, n), 1)
    return jnp.where(c <= r, 1.0, 0.0).astype(dtype)


def _group_norm(y, gain):
    mu = jnp.mean(y, axis=-1, keepdims=True)
    yc = y - mu
    var = jnp.mean(yc * yc, axis=-1, keepdims=True)
    return yc * lax.rsqrt(var + EPS) * gain


def _mod_kernel(c_ref, w_ref, b_ref, o_ref, *, nb, tn):
    w = w_ref[0]
    rows = []
    for b in range(nb):
        cb = _silu(c_ref[b])
        parts = [jnp.sum(w[:, j * LANES:(j + 1) * LANES] * cb, axis=0, keepdims=True)
                 for j in range(tn // LANES)]
        rows.append(jnp.concatenate(parts, axis=1))
    o_ref[0] = jnp.concatenate(rows, axis=0) + b_ref[0]


def _modulation(c, ada_w, ada_b):
    depth, d, n = ada_w.shape
    nb = c.shape[0]
    tn = 512
    c_b = jnp.broadcast_to(c[:, :, None], (nb, d, LANES))
    return pl.pallas_call(
        functools.partial(_mod_kernel, nb=nb, tn=tn),
        grid=(depth, n // tn),
        in_specs=[pl.BlockSpec((nb, d, LANES), lambda l, j: (0, 0, 0)),
                  pl.BlockSpec((1, d, tn), lambda l, j: (l, 0, j)),
                  pl.BlockSpec((1, 1, tn), lambda l, j: (l, 0, j))],
        out_specs=pl.BlockSpec((1, nb, tn), lambda l, j: (l, 0, j)),
        out_shape=jax.ShapeDtypeStruct((depth, nb, n), F32),
        compiler_params=_cparams(2),
        name="adaln_mod",
    )(c_b, ada_w, ada_b.reshape(depth, 1, n))


def _from_token_major(ref, rows, nch):
    return jnp.concatenate([ref[pl.ds(c, rows, stride=nch), :] for c in range(nch)], axis=1)


def _to_token_major(ref, val, nch):
    rows = val.shape[0]
    for c in range(nch):
        ref[pl.ds(c, rows, stride=nch), :] = val[:, c * LANES:(c + 1) * LANES]


def _norm_kernel(*refs, n_add, tok_gate, write_x, side, h_dtype, add_tm, h_tm):
    it = iter(refs)
    x_ref = next(it)
    add_refs = [next(it) for _ in range(n_add)]
    tg_refs = [next(it) for _ in range(n_add)] if tok_gate else []
    gv_ref = next(it) if n_add else None
    g_ref, sc_ref, sh_ref = next(it), next(it), next(it)
    if side:
        ws_hi_ref, ws_lo_ref, bs_ref = next(it), next(it), next(it)
    xo_ref = next(it) if write_x else None
    h_ref = next(it)
    s_ref = next(it) if side else None

    x = x_ref[...]
    tm, d = x.shape
    nch = d // LANES
    if n_add:
        load = (lambda a: _from_token_major(a, tm, nch)) if add_tm else (lambda a: a[...].astype(F32))
        if tok_gate:
            upd = sum(load(a) * tg[:, 0:1] for a, tg in zip(add_refs, tg_refs))
        else:
            upd = sum(load(a) for a in add_refs)
        x = x + gv_ref[0] * upd
    if write_x:
        xo_ref[...] = x
    y = x * lax.rsqrt(jnp.mean(x * x, axis=-1, keepdims=True) + EPS)
    h = (y * g_ref[...]) * (1.0 + sc_ref[0]) + sh_ref[0]
    if h_tm:
        _to_token_major(h_ref, h.astype(h_dtype), nch)
    else:
        h_ref[...] = h.astype(h_dtype)
    if side:
        h_hi = h.astype(BF16)
        s = _dot(h_hi, ws_hi_ref[...])
        if side == "x3":
            h_lo = (h - h_hi.astype(F32)).astype(BF16)
            s = s + _dot(h_hi, ws_lo_ref[...]) + _dot(h_lo, ws_hi_ref[...])
        s_ref[...] = s + bs_ref[...]


def _norm_call(x, adds, tok_gates, gate_vec, g, sc, sh, side_w, side_b, *, seq, write_x, side,
               h_dtype, add_row_offsets=None, add_tm=False, h_tm=False):
    t, d = x.shape
    tm = 256
    nch = d // LANES
    nsb = seq // tm
    n_add = len(adds)
    tok_gate = bool(tok_gates)
    row = lambda i: (i, 0)
    vec = lambda i: (i // nsb, 0, 0)
    args, specs = [x], [pl.BlockSpec((tm, d), row)]
    for k, a in enumerate(adds):
        off = 0 if add_row_offsets is None else add_row_offsets[k] // tm
        args.append(a)
        blk = (tm * nch, LANES) if add_tm else (tm, d)
        specs.append(pl.BlockSpec(blk, lambda i, off=off: (i + off, 0)))
    for tg in tok_gates:
        args.append(tg)
        specs.append(pl.BlockSpec((tm, LANES), row))
    if n_add:
        args.append(gate_vec)
        specs.append(pl.BlockSpec((1, 1, d), vec))
    args += [g.reshape(1, d), sc, sh]
    specs += [pl.BlockSpec((1, d), lambda i: (0, 0)), pl.BlockSpec((1, 1, d), vec),
              pl.BlockSpec((1, 1, d), vec)]
    if side:
        w_hi = side_w.astype(BF16)
        w_lo = (side_w - w_hi.astype(F32)).astype(BF16)
        args += [w_hi, w_lo, side_b]
        specs += [pl.BlockSpec((d, LANES), lambda i: (0, 0)), pl.BlockSpec((d, LANES), lambda i: (0, 0)),
                  pl.BlockSpec((1, LANES), lambda i: (0, 0))]
    out_shape, out_specs = [], []
    if write_x:
        out_shape.append(jax.ShapeDtypeStruct((t, d), F32))
        out_specs.append(pl.BlockSpec((tm, d), row))
    if h_tm:
        out_shape.append(jax.ShapeDtypeStruct((t * nch, LANES), h_dtype))
        out_specs.append(pl.BlockSpec((tm * nch, LANES), row))
    else:
        out_shape.append(jax.ShapeDtypeStruct((t, d), h_dtype))
        out_specs.append(pl.BlockSpec((tm, d), row))
    if side:
        out_shape.append(jax.ShapeDtypeStruct((t, LANES), F32))
        out_specs.append(pl.BlockSpec((tm, LANES), row))
    return pl.pallas_call(
        functools.partial(_norm_kernel, n_add=n_add, tok_gate=tok_gate, write_x=write_x, side=side,
                          h_dtype=h_dtype, add_tm=add_tm, h_tm=h_tm),
        grid=(t // tm,), in_specs=specs, out_specs=out_specs, out_shape=out_shape,
        compiler_params=_cparams(1), name="resid_norm",
    )(*args)


def _mm_kernel(a_ref, w_ref, o_ref):
    o_ref[...] = _dot(a_ref[...], w_ref[0]).astype(o_ref.dtype)


def _matmul(a, w_stack, layer, out_dtype, tm=1024, tn=512):
    m, k = a.shape
    n = w_stack.shape[2]
    tm = min(tm, m)
    return pl.pallas_call(
        _mm_kernel, grid=(m // tm, n // tn),
        in_specs=[pl.BlockSpec((tm, k), lambda i, j: (i, 0)),
                  pl.BlockSpec((1, k, tn), lambda i, j: (layer, 0, j))],
        out_specs=pl.BlockSpec((tm, tn), lambda i, j: (i, j)),
        out_shape=jax.ShapeDtypeStruct((m, n), out_dtype),
        compiler_params=_cparams(2), name="in_proj",
    )(a, w_stack)


def _mm3_kernel(a1_ref, a2_ref, a3_ref, w_ref, o_ref):
    k1, k2 = a1_ref.shape[1], a2_ref.shape[1]
    acc = _dot(a1_ref[...], w_ref[0, 0:k1, :])
    acc = acc + _dot(a2_ref[...], w_ref[0, k1:k1 + k2, :])
    acc = acc + _dot(a3_ref[...], w_ref[0, k1 + k2:, :])
    o_ref[...] = acc.astype(o_ref.dtype)


def _out_proj(y_ret, y_m, y_f, w_stack, layer, out_dtype, tm=1024, tn=512):
    m = y_ret.shape[0]
    _, k, n = w_stack.shape
    tm = min(tm, m)
    return pl.pallas_call(
        _mm3_kernel, grid=(m // tm, n // tn),
        in_specs=[pl.BlockSpec((tm, y_ret.shape[1]), lambda i, j: (i, 0)),
                  pl.BlockSpec((tm, y_m.shape[1]), lambda i, j: (i, 0)),
                  pl.BlockSpec((tm, y_f.shape[1]), lambda i, j: (i, 0)),
                  pl.BlockSpec((1, k, tn), lambda i, j: (layer, 0, j))],
        out_specs=pl.BlockSpec((tm, tn), lambda i, j: (i, j)),
        out_shape=jax.ShapeDtypeStruct((m, n), out_dtype),
        compiler_params=_cparams(2), name="out_proj",
    )(y_ret, y_m, y_f, w_stack)


def _ret_kernel(q_ref, k_ref, v_ref, g_ref, cos_ref, sin_ref, decay_ref, zeta_ref, xi_ref, gam_ref,
                gn_ref, o_ref, state_ref, *, ts):
    @pl.when(pl.program_id(1) == 0)
    def _():
        state_ref[...] = jnp.zeros_like(state_ref)

    for c in range(ts // CHUNK):
        rows = slice(c * CHUNK, (c + 1) * CHUNK)
        cosv, sinv = cos_ref[rows, :], sin_ref[rows, :]
        for h in range(RET_HEADS):
            cols = slice(h * HEAD_DIM, (h + 1) * HEAD_DIM)
            q = q_ref[rows, cols].astype(F32)
            k = k_ref[rows, cols].astype(F32)
            qr = q * cosv + pltpu.roll(q, HEAD_DIM // 2, 1) * sinv
            kr = k * cosv + pltpu.roll(k, HEAD_DIM // 2, 1) * sinv
            qb, kb, vb = qr.astype(BF16), kr.astype(BF16), v_ref[rows, cols]
            s = _dot_nt(qb, kb) * decay_ref[h]
            state = state_ref[h]
            y = _dot(s.astype(BF16), vb) + _dot(qb, state.astype(BF16)) * xi_ref[h]
            kz_t = (kr * zeta_ref[h]).T.astype(BF16)
            state_ref[h] = state * gam_ref[h] + _dot(kz_t, vb)
            out = _group_norm(y, gn_ref[:, cols]) * _silu(g_ref[rows, cols].astype(F32))
            o_ref[rows, cols] = out.astype(o_ref.dtype)


def _retention(proj, gn_g, consts, *, batch, seq):
    ts = 512
    ns = seq // ts
    cosf, sins, decay, zeta, xi, gam = consts
    rowblk = lambda cb: pl.BlockSpec((ts, D_RET), lambda b, s, cb=cb: (b * ns + s, cb))
    full3 = pl.BlockSpec((RET_HEADS, CHUNK, HEAD_DIM), lambda b, s: (0, 0, 0))
    return pl.pallas_call(
        functools.partial(_ret_kernel, ts=ts), grid=(batch, ns),
        in_specs=[rowblk(COL_RQ // D_RET), rowblk(COL_RK // D_RET), rowblk(COL_RV // D_RET),
                  rowblk(COL_RG // D_RET),
                  pl.BlockSpec((ts, HEAD_DIM), lambda b, s: (s, 0)),
                  pl.BlockSpec((ts, HEAD_DIM), lambda b, s: (s, 0)),
                  full3, full3, full3,
                  pl.BlockSpec((RET_HEADS, 1, HEAD_DIM), lambda b, s: (0, 0, 0)),
                  pl.BlockSpec((1, D_RET), lambda b, s: (0, 0))],
        out_specs=pl.BlockSpec((ts, D_RET), lambda b, s: (b * ns + s, 0)),
        out_shape=jax.ShapeDtypeStruct((batch * seq, D_RET), BF16),
        scratch_shapes=[pltpu.VMEM((RET_HEADS, HEAD_DIM, HEAD_DIM), F32)],
        compiler_params=_cparams(2), name="retention",
    )(proj, proj, proj, proj, cosf, sins, decay, zeta, xi, gam, gn_g.reshape(1, D_RET))


def _retention_consts(seq):
    h = RET_HEADS
    log_gamma = jnp.log1p(-jnp.power(2.0, -5.0 - jnp.arange(h, dtype=F32)))
    pos = jnp.arange(CHUNK, dtype=F32)
    diff = pos[:, None] - pos[None, :]
    decay = jnp.where(diff >= 0, jnp.exp(jnp.maximum(diff, 0.0)[None] * log_gamma[:, None, None]), 0.0)
    zeta = jnp.exp((CHUNK - 1 - pos)[None, :] * log_gamma[:, None])
    xi = jnp.exp((pos + 1)[None, :] * log_gamma[:, None])
    gam = jnp.exp(CHUNK * log_gamma)
    bcast = lambda v: jnp.broadcast_to(v[:, :, None], (h, CHUNK, HEAD_DIM))
    half = HEAD_DIM // 2
    inv = ROPE_BASE ** (-jnp.arange(half, dtype=F32) / half)
    ang = jnp.arange(seq, dtype=F32)[:, None] * inv[None, :]
    cosf = jnp.concatenate([jnp.cos(ang), jnp.cos(ang)], axis=1)
    sins = jnp.concatenate([-jnp.sin(ang), jnp.sin(ang)], axis=1)
    gam_b = jnp.broadcast_to(gam[:, None, None], (h, 1, HEAD_DIM))
    return cosf, sins, decay, bcast(zeta), bcast(xi), gam_b


def _mlstm_kernel(x_ref, v_ref, og_ref, gates_ref, cw_ref, cb_ref, wq_ref, wk_ref, gn_ref, o_ref,
                  xbuf, c_state, n_state, m_state, *, ts):
    @pl.when(pl.program_id(1) == 0)
    def _():
        c_state[...] = jnp.zeros_like(c_state)
        n_state[...] = jnp.zeros_like(n_state)
        m_state[...] = jnp.full_like(m_state, NEG_INF)
        xbuf[0:8, :] = jnp.zeros((8, D_MLSTM), F32)

    x = x_ref[...].astype(F32)
    xbuf[8:ts + 8, :] = x
    conv = cb_ref[...]
    for j in range(CONV_WIDTH):
        start = 8 - (CONV_WIDTH - 1) + j
        conv = conv + cw_ref[j:j + 1, :] * xbuf[start:start + ts, :]
    xbuf[0:8, :] = x[ts - 8:ts, :]
    xc = _silu(conv).astype(BF16)

    tri = _lower_tri(CHUNK, BF16)
    r_i = lax.broadcasted_iota(jnp.int32, (CHUNK, CHUNK), 0)
    c_i = lax.broadcasted_iota(jnp.int32, (CHUNK, CHUNK), 1)
    causal = c_i <= r_i

    q_all = [_dot(xc[:, h * HEAD_DIM:(h + 1) * HEAD_DIM], wq_ref[h]).astype(BF16) for h in range(MLSTM_HEADS)]
    k_all = [_dot(xc[:, h * HEAD_DIM:(h + 1) * HEAD_DIM], wk_ref[h]) for h in range(MLSTM_HEADS)]

    lane8 = lax.broadcasted_iota(jnp.int32, (SUBLANES, CHUNK), 1)
    ones_cols = jnp.ones((CHUNK, HEAD_DIM), BF16)
    square = (CHUNK, CHUNK)

    def per_token_cols(row):
        return jnp.broadcast_to(row, square).T

    for c in range(ts // CHUNK):
        rows = slice(c * CHUNK, (c + 1) * CHUNK)
        gc = gates_ref[rows, :]
        b_col = _cumsum_rows(tri, _log_sigmoid(gc))
        li8 = gc.T[LANE_MI:LANE_MI + SUBLANES, :]
        b8 = pltpu.roll(b_col.T[0:SUBLANES, :], SUBLANES - LANE_MF, 0)
        g8 = li8 - b8
        cm8 = g8
        shift = 1
        while shift < CHUNK:
            cm8 = jnp.maximum(cm8, jnp.where(lane8 >= shift, pltpu.roll(cm8, shift, 1), NEG_INF))
            shift *= 2
        m_prev8 = m_state[...]
        big_m8 = jnp.maximum(cm8, m_prev8)
        m_t8 = b8 + big_m8
        b_end8 = jnp.broadcast_to(b8[:, CHUNK - 1:CHUNK], (SUBLANES, CHUNK))
        a8 = b_end8 + g8
        a_max8 = jnp.broadcast_to(jnp.max(a8, axis=1, keepdims=True), (SUBLANES, CHUNK))
        w8 = jnp.exp(a8 - a_max8)
        m_new8 = jnp.maximum(b_end8 + m_prev8, a_max8)
        s_old8 = jnp.exp(b_end8 + m_prev8 - m_new8)
        s_new8 = jnp.exp(a_max8 - m_new8)
        m_state[...] = m_new8

        for h in range(MLSTM_HEADS):
            cols = slice(h * HEAD_DIM, (h + 1) * HEAD_DIM)
            hrow = slice(h, h + 1)
            c_prev, n_prev = c_state[h], n_state[h]
            big_m = per_token_cols(big_m8[hrow, :])
            d_w = jnp.exp(jnp.where(causal, jnp.broadcast_to(g8[hrow, :], square) - big_m, NEG_INF))
            inter_w = jnp.exp(m_prev8[hrow, :] - big_m)
            floor = jnp.exp(-per_token_cols(m_t8[hrow, :]))

            q = q_all[h][rows, :]
            kf = k_all[h][rows, :]
            v_ones = jnp.concatenate([v_ref[rows, cols], ones_cols], axis=1)
            s = _dot_nt(q, kf.astype(BF16)) * d_w
            sv = _dot(s.astype(BF16), v_ones)
            qc = _dot(q, jnp.concatenate([c_prev, n_prev], axis=1).astype(BF16))
            num = sv[:, :HEAD_DIM] + inter_w * qc[:, :HEAD_DIM]
            den = sv[:, HEAD_DIM:] + inter_w * qc[:, HEAD_DIM:]
            hh = num / jnp.maximum(jnp.abs(den), floor)

            kw_t = (kf.T * w8[hrow, :]).astype(BF16)
            kv = _dot(kw_t, v_ones)
            c_state[h] = s_old8[hrow, :] * c_prev + s_new8[hrow, :] * kv[:, :HEAD_DIM]
            n_state[h] = s_old8[hrow, :] * n_prev + s_new8[hrow, :] * kv[:, HEAD_DIM:]

            out = jax.nn.sigmoid(og_ref[rows, cols].astype(F32)) * _group_norm(hh, gn_ref[:, cols])
            o_ref[rows, cols] = out.astype(o_ref.dtype)


def _mlstm(proj, gates, conv_w, conv_b, wq, wk, gn_g, *, batch, seq):
    ts = 512
    ns = seq // ts
    rowblk = lambda cb: pl.BlockSpec((ts, D_MLSTM), lambda b, s, cb=cb: (b * ns + s, cb))
    wspec = pl.BlockSpec((MLSTM_HEADS, HEAD_DIM, HEAD_DIM), lambda b, s: (0, 0, 0))
    return pl.pallas_call(
        functools.partial(_mlstm_kernel, ts=ts), grid=(batch, ns),
        in_specs=[rowblk(COL_MX // D_MLSTM), rowblk(COL_MV // D_MLSTM), rowblk(COL_MO // D_MLSTM),
                  pl.BlockSpec((ts, LANES), lambda b, s: (b * ns + s, 0)),
                  pl.BlockSpec((CONV_WIDTH, D_MLSTM), lambda b, s: (0, 0)),
                  pl.BlockSpec((1, D_MLSTM), lambda b, s: (0, 0)),
                  wspec, wspec,
                  pl.BlockSpec((1, D_MLSTM), lambda b, s: (0, 0))],
        out_specs=pl.BlockSpec((ts, D_MLSTM), lambda b, s: (b * ns + s, 0)),
        out_shape=jax.ShapeDtypeStruct((batch * seq, D_MLSTM), BF16),
        scratch_shapes=[pltpu.VMEM((ts + 8, D_MLSTM), F32),
                        pltpu.VMEM((MLSTM_HEADS, HEAD_DIM, HEAD_DIM), F32),
                        pltpu.VMEM((MLSTM_HEADS, HEAD_DIM, HEAD_DIM), F32),
                        pltpu.VMEM((SUBLANES, CHUNK), F32)],
        compiler_params=_cparams(2), name="mlstm",
    )(proj, proj, proj, gates, conv_w, conv_b.reshape(1, D_MLSTM), wq.astype(BF16),
      (wk * HEAD_DIM ** -0.5).astype(BF16), gn_g.reshape(1, D_MLSTM))


def _fcum_kernel(gates_ref, fcol_ref, frow_ref, carry_ref, *, ts):
    @pl.when(pl.program_id(1) == 0)
    def _():
        carry_ref[...] = jnp.zeros_like(carry_ref)

    tri = _lower_tri(CHUNK, BF16)
    carry = carry_ref[...]
    for c in range(ts // CHUNK):
        rows = slice(c * CHUNK, (c + 1) * CHUNK)
        cum = _cumsum_rows(tri, _log_sigmoid(gates_ref[rows, :])) + carry
        carry = cum[CHUNK - 1:CHUNK, :]
        fcol_ref[rows, :] = cum
        frow_ref[0, :, rows] = cum.T[LANE_FF:LANE_FF + FOX_HEADS, :]
    carry_ref[...] = carry


def _forget_cumsum(gates, *, batch, seq):
    ts = 512
    ns = seq // ts
    return pl.pallas_call(
        functools.partial(_fcum_kernel, ts=ts), grid=(batch, ns),
        in_specs=[pl.BlockSpec((ts, LANES), lambda b, s: (b * ns + s, 0))],
        out_specs=[pl.BlockSpec((ts, LANES), lambda b, s: (b * ns + s, 0)),
                   pl.BlockSpec((1, FOX_HEADS, ts), lambda b, s: (b, 0, s))],
        out_shape=[jax.ShapeDtypeStruct((batch * seq, LANES), F32),
                   jax.ShapeDtypeStruct((batch, FOX_HEADS, seq), F32)],
        scratch_shapes=[pltpu.VMEM((1, LANES), F32)],
        compiler_params=_cparams(2), name="forget_cumsum",
    )(gates)


def _fox_kernel(q_ref, k_ref, v_ref, fcol_ref, frow_ref, o_ref, s_ref, acc_ref, m_ref, l_ref, *, tq):
    h = pl.program_id(1)
    i = pl.program_id(2)
    q = q_ref[...]
    tk = tq
    lane = lax.broadcasted_iota(jnp.int32, (tq, LANES), 1)
    f_q = jnp.sum(jnp.where(lane == LANE_FF + h, fcol_ref[...], 0.0), axis=1, keepdims=True)
    f_q = jnp.broadcast_to(f_q, (tq, LANES))
    lane_tiles = lambda a: [a[:, c * LANES:(c + 1) * LANES] for c in range(a.shape[1] // LANES)]

    def scores(start, width):
        return _dot_nt(q, k_ref[pl.ds(start, width), :]) - frow_ref[0, :, pl.ds(start, width)]

    def keep(start, s):
        s_ref[:, pl.ds(start, s.shape[1])] = s
        m = m_ref[...]
        for s_c in lane_tiles(s):
            m = jnp.maximum(m, s_c)
        m_ref[...] = m

    def weigh(start, width):
        ps = [jnp.exp(s_c + shift) for s_c in lane_tiles(s_ref[:, pl.ds(start, width)])]
        l_ref[...] = l_ref[...] + sum(ps[1:], ps[0])
        p = jnp.concatenate(ps, axis=1).astype(BF16)
        acc_ref[...] = acc_ref[...] + _dot(p, v_ref[pl.ds(start, width), :])

    def sweep(n_keys, fn):
        n_wide = n_keys // tk

        def body(j, carry):
            fn(pl.multiple_of(j * tk, tk), tk)
            return carry

        lax.fori_loop(0, n_wide, body, 0)

        @pl.when(n_wide * tk < n_keys)
        def _():
            fn(pl.multiple_of(n_wide * tk, tq), tq)

    m_ref[...] = jnp.full_like(m_ref, NEG_INF)
    diag = pl.multiple_of(i * tq, tq)
    sweep(diag, lambda start, width: keep(start, scores(start, width)))
    r_i = lax.broadcasted_iota(jnp.int32, (tq, tq), 0)
    c_i = lax.broadcasted_iota(jnp.int32, (tq, tq), 1)
    keep(diag, jnp.where(c_i <= r_i, scores(diag, tq), NEG_INF))

    m_row = jnp.max(m_ref[...], axis=1, keepdims=True) + f_q
    shift = f_q - m_row
    acc_ref[...] = jnp.zeros_like(acc_ref)
    l_ref[...] = jnp.zeros_like(l_ref)
    sweep(diag + tq, weigh)
    l_row = jnp.sum(l_ref[...], axis=1, keepdims=True)
    o_ref[...] = (acc_ref[...] / l_row).astype(o_ref.dtype)


def _fox_attention(proj, fcol, frow, *, batch, seq):
    tq = min(1024, seq)
    nq = seq // tq
    cq, ck, cv = COL_FQ // HEAD_DIM, COL_FK // HEAD_DIM, COL_FV // HEAD_DIM
    return pl.pallas_call(
        functools.partial(_fox_kernel, tq=tq), grid=(batch, FOX_HEADS, nq),
        in_specs=[pl.BlockSpec((tq, HEAD_DIM), lambda b, h, i: (b * nq + i, cq + h)),
                  pl.BlockSpec((seq, HEAD_DIM), lambda b, h, i: (b, ck + h)),
                  pl.BlockSpec((seq, HEAD_DIM), lambda b, h, i: (b, cv + h)),
                  pl.BlockSpec((tq, LANES), lambda b, h, i: (b * nq + i, 0)),
                  pl.BlockSpec((1, 1, seq), lambda b, h, i: (b * FOX_HEADS + h, 0, 0))],
        out_specs=pl.BlockSpec((tq, HEAD_DIM), lambda b, h, i: (b * nq + i, h)),
        out_shape=jax.ShapeDtypeStruct((batch * seq, D_FOX), BF16),
        scratch_shapes=[pltpu.VMEM((tq, seq), F32), pltpu.VMEM((tq, HEAD_DIM), F32),
                        pltpu.VMEM((tq, LANES), F32), pltpu.VMEM((tq, LANES), F32)],
        compiler_params=_cparams(3), name="fox_attention",
    )(proj, proj, proj, fcol, frow.reshape(batch * FOX_HEADS, 1, seq))


def _route_kernel(logit_ref, eid_ref, gate_ref, rank_ref, cnt_ref, upper_ref, base_ref, *, tm):
    @pl.when(pl.program_id(0) == 0)
    def _():
        r = lax.broadcasted_iota(jnp.int32, (tm, tm), 0)
        c = lax.broadcasted_iota(jnp.int32, (tm, tm), 1)
        upper_ref[...] = jnp.where(r < c, 1.0, 0.0).astype(BF16)
        base_ref[...] = jnp.zeros_like(base_ref)

    lt = logit_ref[...].T
    e = lt[LANE_RE:LANE_RE + N_EXPERTS, :]
    g = lt[LANE_RG:LANE_RG + N_GROUPS, :]
    g_max = jnp.max(g, axis=0, keepdims=True)
    g_val = 1.0 / jnp.sum(jnp.exp(g - g_max), axis=0, keepdims=True)
    g_row = lax.broadcasted_iota(jnp.int32, (N_GROUPS, tm), 0).astype(F32)
    g_idx = jnp.min(jnp.where(g == g_max, g_row, float(N_GROUPS)), axis=0, keepdims=True)

    e_row_i = lax.broadcasted_iota(jnp.int32, (N_EXPERTS, tm), 0)
    e_row = e_row_i.astype(F32)
    e_grp = jnp.right_shift(e_row_i, EXPERTS_PER_GROUP.bit_length() - 1).astype(F32)
    el = jnp.where(e_grp == g_idx, e, NEG_INF)
    max1 = jnp.max(el, axis=0, keepdims=True)
    idx1 = jnp.min(jnp.where(el == max1, e_row, float(N_EXPERTS)), axis=0, keepdims=True)
    el2 = jnp.where(e_row == idx1, NEG_INF, el)
    max2 = jnp.max(el2, axis=0, keepdims=True)
    idx2 = jnp.min(jnp.where(el2 == max2, e_row, float(N_EXPERTS)), axis=0, keepdims=True)
    e_sum = jnp.sum(jnp.exp(el - max1), axis=0, keepdims=True)
    p1 = 1.0 / e_sum
    p2 = jnp.exp(max2 - max1) / e_sum
    p_sum = p1 + p2
    gate_ref[0:1, :] = g_val * (p1 / p_sum)
    gate_ref[1:2, :] = g_val * (p2 / p_sum)
    eid_ref[0:1, :] = idx1.astype(jnp.int32)
    eid_ref[1:2, :] = idx2.astype(jnp.int32)

    oh1 = e_row == idx1
    oh2 = e_row == idx2
    onehot = jnp.where(oh1 | oh2, 1.0, 0.0)
    before = _dot(onehot.astype(BF16), upper_ref[...]) + base_ref[:, 0:1]
    rank_ref[0:1, :] = jnp.sum(jnp.where(oh1, before, 0.0), axis=0, keepdims=True).astype(jnp.int32)
    rank_ref[1:2, :] = jnp.sum(jnp.where(oh2, before, 0.0), axis=0, keepdims=True).astype(jnp.int32)
    base_ref[...] = base_ref[...] + jnp.sum(onehot, axis=1, keepdims=True)
    cnt_ref[...] = base_ref[...]


def _route(logits):
    t = logits.shape[0]
    tm = min(1024, t)
    tok = lambda i: (0, i)
    return pl.pallas_call(
        functools.partial(_route_kernel, tm=tm), grid=(t // tm,),
        in_specs=[pl.BlockSpec((tm, LANES), lambda i: (i, 0))],
        out_specs=[pl.BlockSpec((2, tm), tok), pl.BlockSpec((2, tm), tok), pl.BlockSpec((2, tm), tok),
                   pl.BlockSpec((N_EXPERTS, LANES), lambda i: (0, 0))],
        out_shape=[jax.ShapeDtypeStruct((2, t), jnp.int32), jax.ShapeDtypeStruct((2, t), F32),
                   jax.ShapeDtypeStruct((2, t), jnp.int32),
                   jax.ShapeDtypeStruct((N_EXPERTS, LANES), F32)],
        scratch_shapes=[pltpu.VMEM((tm, tm), BF16), pltpu.VMEM((N_EXPERTS, LANES), F32)],
        compiler_params=_cparams(1), name="route",
    )(logits)


def _moe_kernel(tile_expert, n_active, tile_rows, row_id, h_hbm, w1_ref, w3_ref, w2_ref, o_hbm,
                xbuf, xb, ybuf, gsem, ssem, *, tm, nch, n_tokens, n_tiles):
    del tile_expert
    i = pl.program_id(0)
    n_act = n_active[0]
    last_step = jnp.minimum(n_act, n_tiles - 1)

    def token_of(rid):
        if n_tokens & (n_tokens - 1) == 0:
            return jnp.bitwise_and(rid, n_tokens - 1)
        return rid - n_tokens * ((rid >= n_tokens).astype(jnp.int32) + (rid >= 2 * n_tokens).astype(jnp.int32))

    def hbm_token(ref, tok):
        return ref.at[pl.ds(pl.multiple_of(tok * nch, nch), nch)]

    def vmem_row(ref, r):
        if isinstance(r, int):
            return ref.at[r // SUBLANES, :, r % SUBLANES, :]
        return ref.at[r // SUBLANES, :, lax.rem(r, SUBLANES), :]

    def start_gather(tile, r):
        tok = token_of(row_id[(tile + 1) * tm + r])
        pltpu.make_async_copy(hbm_token(h_hbm, tok), vmem_row(xbuf, r), gsem).start()

    def start_scatter(tile, r):
        dst = row_id[(tile + 1) * tm + r]
        pltpu.make_async_copy(vmem_row(ybuf, r), hbm_token(o_hbm, dst), ssem).start(priority=1)

    def wait_one(sem):
        pltpu.make_async_copy(hbm_token(h_hbm, 0), vmem_row(xbuf, 0), sem).wait()

    def for_rows(n, fn):
        @pl.when(n == tm)
        def _():
            for r in range(tm):
                fn(r)

        @pl.when(n < tm)
        def _():
            n_groups = n // SUBLANES

            def group(g, carry):
                for u in range(SUBLANES):
                    fn(g * SUBLANES + u)
                return carry

            lax.fori_loop(0, n_groups, group, 0)
            for u in range(SUBLANES - 1):
                @pl.when(n_groups * SUBLANES + u < n)
                def _():
                    fn(n_groups * SUBLANES + u)

    def wait_rows(sem, n):
        for_rows(n, lambda r: wait_one(sem))

    rows_prev, rows_cur = tile_rows[i], tile_rows[i + 1]
    nxt = jnp.minimum(i + 1, n_tiles - 1)
    rows_next = tile_rows[nxt + 1]

    @pl.when(i <= last_step)
    def _():
        @pl.when(i == 0)
        def _():
            xbuf[...] = jnp.zeros_like(xbuf)
            ybuf[...] = jnp.zeros_like(ybuf)
            for_rows(rows_cur, lambda r: start_gather(0, r))

        wait_rows(gsem, rows_cur)
        xb[...] = jnp.concatenate([xbuf[:, c].reshape(tm, LANES) for c in range(nch)], axis=1).astype(BF16)
        for_rows(rows_next, lambda r: start_gather(nxt, r))
        for_rows(rows_prev, lambda r: start_scatter(i - 1, r))

        @pl.when(n_act >= i)
        def _():
            x = xb[...]
            h1 = _dot(x, w1_ref[0, 0].astype(BF16))
            h3 = _dot(x, w3_ref[0, 0].astype(BF16))
            act = (_silu(h1) * h3).astype(BF16)
            y = _dot(act, w2_ref[0, 0].astype(BF16))
            wait_rows(ssem, rows_prev)
            for c in range(nch):
                ybuf[:, c] = y[:, c * LANES:(c + 1) * LANES].reshape(tm // SUBLANES, SUBLANES, LANES)

        @pl.when(i == last_step)
        def _():
            wait_rows(gsem, rows_next)

            @pl.when(i < n_act)
            def _():
                for_rows(rows_cur, lambda r: start_scatter(i, r))
                wait_rows(ssem, rows_cur)


def _moe_experts(h2_tm, w1, w3, w2, layer, tile_expert, n_active, tile_rows, row_id, *, n_tiles):
    _, _, d, de = w1.shape
    nch = d // LANES
    t = h2_tm.shape[0] // nch
    tm = MOE_TILE
    wmap = lambda i, te, na, tr, rid: (layer, te[i], 0, 0)
    grid_spec = pltpu.PrefetchScalarGridSpec(
        num_scalar_prefetch=4, grid=(n_tiles,),
        in_specs=[pl.BlockSpec(memory_space=pl.ANY),
                  pl.BlockSpec((1, 1, d, de), wmap), pl.BlockSpec((1, 1, d, de), wmap),
                  pl.BlockSpec((1, 1, de, d), wmap)],
        out_specs=pl.BlockSpec(memory_space=pl.ANY),
        scratch_shapes=[pltpu.VMEM((tm // SUBLANES, nch, SUBLANES, LANES), F32), pltpu.VMEM((tm, d), BF16),
                        pltpu.VMEM((tm // SUBLANES, nch, SUBLANES, LANES), F32),
                        pltpu.SemaphoreType.DMA(()), pltpu.SemaphoreType.DMA(())])
    return pl.pallas_call(
        functools.partial(_moe_kernel, tm=tm, nch=nch, n_tokens=t, n_tiles=n_tiles), grid_spec=grid_spec,
        out_shape=jax.ShapeDtypeStruct(((2 * t + tm) * nch, LANES), F32),
        compiler_params=_cparams(1), name="moe_experts",
    )(tile_expert, n_active, tile_rows, row_id, h2_tm, w1, w3, w2)


SMEM_1D_TILE = 1024


def _rowid_kernel(dest_ref, spare_hbm, o_ref, sem, *, n_pairs, tm):
    fill = pltpu.make_async_copy(spare_hbm, o_ref, sem)
    fill.start()
    fill.wait()

    def body(p, carry):
        o_ref[tm + dest_ref[p]] = p
        return carry

    lax.fori_loop(0, n_pairs, body, 0, unroll=8)


def _sorted_row_ids(dest, *, n_tokens, n_tiles):
    tm = MOE_TILE
    n_rows = -(-(n_tiles + 1) * tm // SMEM_1D_TILE) * SMEM_1D_TILE
    spare = 2 * n_tokens + jnp.arange(n_rows, dtype=jnp.int32) % tm
    return pl.pallas_call(
        functools.partial(_rowid_kernel, n_pairs=dest.shape[0], tm=tm),
        in_specs=[pl.BlockSpec(memory_space=pltpu.SMEM), pl.BlockSpec(memory_space=pl.ANY)],
        out_specs=pl.BlockSpec(memory_space=pltpu.SMEM),
        out_shape=jax.ShapeDtypeStruct((n_rows,), jnp.int32),
        scratch_shapes=[pltpu.SemaphoreType.DMA(())],
        name="sorted_row_ids",
    )(dest, spare)


def _dispatch_plan(eid, rank, counts, *, n_tokens, n_tiles):
    tm = MOE_TILE
    cnt = counts[:, 0].astype(jnp.int32)
    tiles_e = (cnt + tm - 1) // tm
    tile_end = jnp.cumsum(tiles_e)
    tile_start = tile_end - tiles_e
    experts = jnp.arange(N_EXPERTS, dtype=jnp.int32)
    row_off = jnp.sum(jnp.where(eid[..., None] == experts, tile_start * tm, 0), axis=-1)
    dest = row_off + rank
    n_active = tile_end[-1]
    tile_ids = jnp.arange(n_tiles, dtype=jnp.int32)
    last_tile = jnp.minimum(tile_ids, n_active - 1)
    tile_expert = jnp.sum((tile_end[None, :] <= last_tile[:, None]).astype(jnp.int32), axis=-1)
    tile_expert = jnp.minimum(tile_expert, N_EXPERTS - 1)
    row_id = _sorted_row_ids(dest.reshape(-1), n_tokens=n_tokens, n_tiles=n_tiles)
    first_tile = jnp.sum(jnp.where(tile_expert[:, None] == experts, tile_start, 0), axis=-1)
    held = jnp.sum(jnp.where(tile_expert[:, None] == experts, cnt, 0), axis=-1) - (tile_ids - first_tile) * tm
    held = jnp.where(tile_ids < n_active, jnp.clip(held, 0, tm), 0)
    tile_rows = jnp.concatenate([jnp.full((1,), tm, jnp.int32), held.astype(jnp.int32)])
    return tile_expert, n_active.reshape(1), tile_rows, row_id


W_IN_GATES0 = COL_MO + D_MLSTM
W_IN_FOX0 = W_IN_GATES0 + 2 * MLSTM_HEADS
W_IN_GATES1 = W_IN_FOX0 + 3 * D_FOX


def _wprep_kernel(w_ref, o_ref, g_ref):
    s = HEAD_DIM ** -0.5
    w = w_ref[0]

    def put(dst, src):
        o_ref[0, :, dst:dst + src.shape[1]] = src.astype(BF16)

    put(COL_RQ, w[:, 0:COL_RK])
    put(COL_RK, w[:, COL_RK:COL_RV] * s)
    put(COL_RV, w[:, COL_RV:W_IN_GATES0])
    put(COL_FQ, w[:, W_IN_FOX0:W_IN_FOX0 + D_FOX] * s)
    put(COL_FK, w[:, W_IN_FOX0 + D_FOX:W_IN_GATES1])
    n_gate = 2 * MLSTM_HEADS + FOX_HEADS
    g_ref[0] = jnp.concatenate([w[:, W_IN_GATES0:W_IN_FOX0], w[:, W_IN_GATES1:W_IN_GATES1 + FOX_HEADS],
                                jnp.zeros((w.shape[0], LANES - n_gate), F32)], axis=1)


def _rearranged_w_in(w_in):
    depth, d, n_in = w_in.shape
    tr = 256
    return pl.pallas_call(
        _wprep_kernel, grid=(depth, d // tr),
        in_specs=[pl.BlockSpec((1, tr, n_in), lambda l, r: (l, r, 0))],
        out_specs=[pl.BlockSpec((1, tr, N_PROJ), lambda l, r: (l, r, 0)),
                   pl.BlockSpec((1, tr, LANES), lambda l, r: (l, r, 0))],
        out_shape=[jax.ShapeDtypeStruct((depth, d, N_PROJ), BF16),
                   jax.ShapeDtypeStruct((depth, d, LANES), F32)],
        compiler_params=_cparams(2), name="w_in_prep",
    )(w_in)


def _lane_pad(v):
    return jnp.pad(v, ((0, 0), (0, LANES - v.shape[-1])))


def kernel(x, c, ada_w, ada_b, norm1_g, w_in, ret_gn_g, mlstm_conv_w, mlstm_conv_b, mlstm_wq, mlstm_wk,
           mlstm_i_b, mlstm_f_b, mlstm_gn_g, fox_f_b, w_out, norm2_g, router_group_w, router_group_b,
           router_expert_w, router_expert_b, moe_w1, moe_w3, moe_w2, final_g):
    batch, seq, d = x.shape
    depth = ada_w.shape[0]
    t = batch * seq
    n_tiles = (2 * t) // MOE_TILE + N_EXPERTS

    mod = _modulation(c, ada_w, ada_b)
    mod = mod.reshape(depth, batch, N_MOD, 1, d)
    w_big, w_gate = _rearranged_w_in(w_in)
    gate_bias = _lane_pad(jnp.concatenate([mlstm_i_b, mlstm_f_b, fox_f_b], axis=-1))
    w_route = jnp.concatenate([router_expert_w, router_group_w], axis=-1)
    w_route = jnp.pad(w_route, ((0, 0), (0, 0), (0, LANES - w_route.shape[-1])))
    route_bias = _lane_pad(jnp.concatenate([router_expert_b, router_group_b], axis=-1))
    w_out_b = w_out.astype(BF16)
    ret_consts = _retention_consts(seq)

    xf = x.reshape(t, d)
    moe_out, tok_gates, g2_prev = None, (), None
    for l in range(depth):
        sh1, sc1, g1, sh2, sc2, g2 = (mod[l, :, k] for k in range(N_MOD))
        if l == 0:
            h1, gates = _norm_call(xf, (), (), None, norm1_g[l], sc1, sh1, w_gate[l], gate_bias[l:l + 1],
                                   seq=seq, write_x=False, side="x1", h_dtype=BF16)
        else:
            xf, h1, gates = _norm_call(xf, (moe_out, moe_out), tok_gates, g2_prev, norm1_g[l], sc1, sh1,
                                       w_gate[l], gate_bias[l:l + 1], seq=seq, write_x=True, side="x1",
                                       h_dtype=BF16, add_row_offsets=(0, t), add_tm=True)
        proj = _matmul(h1, w_big, l, BF16)
        y_ret = _retention(proj, ret_gn_g[l], ret_consts, batch=batch, seq=seq)
        y_m = _mlstm(proj, gates, mlstm_conv_w[l], mlstm_conv_b[l], mlstm_wq[l], mlstm_wk[l],
                     mlstm_gn_g[l], batch=batch, seq=seq)
        fcol, frow = _forget_cumsum(gates, batch=batch, seq=seq)
        y_f = _fox_attention(proj, fcol, frow, batch=batch, seq=seq)
        mix = _out_proj(y_ret, y_m, y_f, w_out_b, l, BF16)
        xf, h2, logits = _norm_call(xf, (mix,), (), g1, norm2_g[l], sc2, sh2, w_route[l],
                                    route_bias[l:l + 1], seq=seq, write_x=True, side="x3", h_dtype=F32,
                                    h_tm=True)
        eid, gate, rank, counts = _route(logits)
        tile_expert, n_active, tile_rows, row_id = _dispatch_plan(eid, rank, counts, n_tokens=t,
                                                                  n_tiles=n_tiles)
        moe_out = _moe_experts(h2, moe_w1, moe_w3, moe_w2, l, tile_expert, n_active, tile_rows, row_id,
                               n_tiles=n_tiles)
        tok_gates = tuple(jnp.broadcast_to(gate[k][:, None], (t, LANES)) for k in range(2))
        g2_prev = g2
    zeros = jnp.zeros((batch, 1, d), F32)
    (out,) = _norm_call(xf, (moe_out, moe_out), tok_gates, g2_prev, final_g, zeros, zeros, None, None,
                        seq=seq, write_x=False, side=None, h_dtype=F32,
                        add_row_offsets=(0, t), add_tm=True)
    return out.reshape(batch, seq, d)
```

```python
import functools

import jax
import jax.numpy as jnp
from jax import lax
from jax.experimental import pallas as pl
from jax.experimental.pallas import tpu as pltpu

F32 = jnp.float32
BF16 = jnp.bfloat16

HEAD_DIM = 128
LANES = 128
SUBLANES = 8
RET_HEADS = 4
MLSTM_HEADS = 4
FOX_HEADS = 8
D_RET = RET_HEADS * HEAD_DIM
D_MLSTM = MLSTM_HEADS * HEAD_DIM
D_FOX = FOX_HEADS * HEAD_DIM
CHUNK = 128
CONV_WIDTH = 4
ROPE_BASE = 10000.0
N_GROUPS = 4
EXPERTS_PER_GROUP = 8
N_EXPERTS = N_GROUPS * EXPERTS_PER_GROUP
N_MOD = 6
EPS = 1e-6
NEG_INF = float("-inf")

COL_RQ, COL_RK, COL_RV, COL_RG = 0, 512, 1024, 1536
COL_MX, COL_MV, COL_MO = 2048, 2560, 3072
COL_FQ, COL_FK, COL_FV = 3584, 4608, 5632
N_PROJ = 6656
LANE_MI, LANE_MF, LANE_FF = 0, 4, 8
LANE_RE, LANE_RG = 0, 32

VMEM_LIMIT = 56 * 1024 * 1024
MOE_TILE = 512


def _cparams(n_grid):
    return pltpu.CompilerParams(dimension_semantics=("arbitrary",) * n_grid,
                                vmem_limit_bytes=VMEM_LIMIT)


def _dot(a, b):
    return jnp.dot(a, b, preferred_element_type=F32)


def _dot_nt(a, b):
    return lax.dot_general(a, b, (((1,), (1,)), ((), ())), preferred_element_type=F32)


def _silu(x):
    return x * jax.nn.sigmoid(x)


def _log_sigmoid(x):
    return jnp.minimum(x, 0.0) - jnp.log(1.0 + jnp.exp(-jnp.abs(x)))


def _split3(x):
    hi = x.astype(BF16)
    r1 = x - hi.astype(F32)
    mid = r1.astype(BF16)
    lo = (r1 - mid.astype(F32)).astype(BF16)
    return hi, mid, lo


def _cumsum_rows(tri, x):
    hi, mid, lo = _split3(x)
    return _dot(tri, hi) + _dot(tri, mid) + _dot(tri, lo)


def _lower_tri(n, dtype):
    r = lax.broadcasted_iota(jnp.int32, (n, n), 0)
    c = lax.broadcasted_iota(jnp.int32, (n, n), 1)
    return jnp.where(c <= r, 1.0, 0.0).astype(dtype)


def _group_norm(y, gain):
    mu = jnp.mean(y, axis=-1, keepdims=True)
    yc = y - mu
    var = jnp.mean(yc * yc, axis=-1, keepdims=True)
    return yc * lax.rsqrt(var + EPS) * gain


def _mod_kernel(c_ref, w_ref, b_ref, o_ref, *, nb, tn):
    w = w_ref[0]
    rows = []
    for b in range(nb):
        cb = _silu(c_ref[b])
        parts = [jnp.sum(w[:, j * LANES:(j + 1) * LANES] * cb, axis=0, keepdims=True)
                 for j in range(tn // LANES)]
        rows.append(jnp.concatenate(parts, axis=1))
    o_ref[0] = jnp.concatenate(rows, axis=0) + b_ref[0]


def _modulation(c, ada_w, ada_b):
    depth, d, n = ada_w.shape
    nb = c.shape[0]
    tn = 512
    c_b = jnp.broadcast_to(c[:, :, None], (nb, d, LANES))
    return pl.pallas_call(
        functools.partial(_mod_kernel, nb=nb, tn=tn),
        grid=(depth, n // tn),
        in_specs=[pl.BlockSpec((nb, d, LANES), lambda l, j: (0, 0, 0)),
                  pl.BlockSpec((1, d, tn), lambda l, j: (l, 0, j)),
                  pl.BlockSpec((1, 1, tn), lambda l, j: (l, 0, j))],
        out_specs=pl.BlockSpec((1, nb, tn), lambda l, j: (l, 0, j)),
        out_shape=jax.ShapeDtypeStruct((depth, nb, n), F32),
        compiler_params=_cparams(2),
        name="adaln_mod",
    )(c_b, ada_w, ada_b.reshape(depth, 1, n))


def _from_token_major(ref, rows, nch):
    return jnp.concatenate([ref[pl.ds(c, rows, stride=nch), :] for c in range(nch)], axis=1)


def _to_token_major(ref, val, nch):
    rows = val.shape[0]
    for c in range(nch):
        ref[pl.ds(c, rows, stride=nch), :] = val[:, c * LANES:(c + 1) * LANES]


def _norm_kernel(*refs, n_add, tok_gate, write_x, side, h_dtype, add_tm, h_tm):
    it = iter(refs)
    x_ref = next(it)
    add_refs = [next(it) for _ in range(n_add)]
    tg_refs = [next(it) for _ in range(n_add)] if tok_gate else []
    gv_ref = next(it) if n_add else None
    g_ref, sc_ref, sh_ref = next(it), next(it), next(it)
    if side:
        ws_hi_ref, ws_lo_ref, bs_ref = next(it), next(it), next(it)
    xo_ref = next(it) if write_x else None
    h_ref = next(it)
    s_ref = next(it) if side else None

    x = x_ref[...]
    tm, d = x.shape
    nch = d // LANES
    if n_add:
        load = (lambda a: _from_token_major(a, tm, nch)) if add_tm else (lambda a: a[...].astype(F32))
        if tok_gate:
            upd = sum(load(a) * tg[:, 0:1] for a, tg in zip(add_refs, tg_refs))
        else:
            upd = sum(load(a) for a in add_refs)
        x = x + gv_ref[0] * upd
    if write_x:
        xo_ref[...] = x
    y = x * lax.rsqrt(jnp.mean(x * x, axis=-1, keepdims=True) + EPS)
    h = (y * g_ref[...]) * (1.0 + sc_ref[0]) + sh_ref[0]
    if h_tm:
        _to_token_major(h_ref, h.astype(h_dtype), nch)
    else:
        h_ref[...] = h.astype(h_dtype)
    if side:
        h_hi = h.astype(BF16)
        s = _dot(h_hi, ws_hi_ref[...])
        if side == "x3":
            h_lo = (h - h_hi.astype(F32)).astype(BF16)
            s = s + _dot(h_hi, ws_lo_ref[...]) + _dot(h_lo, ws_hi_ref[...])
        s_ref[...] = s + bs_ref[...]


def _norm_call(x, adds, tok_gates, gate_vec, g, sc, sh, side_w, side_b, *, seq, write_x, side,
               h_dtype, add_row_offsets=None, add_tm=False, h_tm=False):
    t, d = x.shape
    tm = 256
    nch = d // LANES
    nsb = seq // tm
    n_add = len(adds)
    tok_gate = bool(tok_gates)
    row = lambda i: (i, 0)
    vec = lambda i: (i // nsb, 0, 0)
    args, specs = [x], [pl.BlockSpec((tm, d), row)]
    for k, a in enumerate(adds):
        off = 0 if add_row_offsets is None else add_row_offsets[k] // tm
        args.append(a)
        blk = (tm * nch, LANES) if add_tm else (tm, d)
        specs.append(pl.BlockSpec(blk, lambda i, off=off: (i + off, 0)))
    for tg in tok_gates:
        args.append(tg)
        specs.append(pl.BlockSpec((tm, LANES), row))
    if n_add:
        args.append(gate_vec)
        specs.append(pl.BlockSpec((1, 1, d), vec))
    args += [g.reshape(1, d), sc, sh]
    specs += [pl.BlockSpec((1, d), lambda i: (0, 0)), pl.BlockSpec((1, 1, d), vec),
              pl.BlockSpec((1, 1, d), vec)]
    if side:
        w_hi = side_w.astype(BF16)
        w_lo = (side_w - w_hi.astype(F32)).astype(BF16)
        args += [w_hi, w_lo, side_b]
        specs += [pl.BlockSpec((d, LANES), lambda i: (0, 0)), pl.BlockSpec((d, LANES), lambda i: (0, 0)),
                  pl.BlockSpec((1, LANES), lambda i: (0, 0))]
    out_shape, out_specs = [], []
    if write_x:
        out_shape.append(jax.ShapeDtypeStruct((t, d), F32))
        out_specs.append(pl.BlockSpec((tm, d), row))
    if h_tm:
        out_shape.append(jax.ShapeDtypeStruct((t * nch, LANES), h_dtype))
        out_specs.append(pl.BlockSpec((tm * nch, LANES), row))
    else:
        out_shape.append(jax.ShapeDtypeStruct((t, d), h_dtype))
        out_specs.append(pl.BlockSpec((tm, d), row))
    if side:
        out_shape.append(jax.ShapeDtypeStruct((t, LANES), F32))
        out_specs.append(pl.BlockSpec((tm, LANES), row))
    return pl.pallas_call(
        functools.partial(_norm_kernel, n_add=n_add, tok_gate=tok_gate, write_x=write_x, side=side,
                          h_dtype=h_dtype, add_tm=add_tm, h_tm=h_tm),
        grid=(t // tm,), in_specs=specs, out_specs=out_specs, out_shape=out_shape,
        compiler_params=_cparams(1), name="resid_norm",
    )(*args)


def _mm_kernel(a_ref, w_ref, o_ref):
    o_ref[...] = _dot(a_ref[...], w_ref[0]).astype(o_ref.dtype)


def _matmul(a, w_stack, layer, out_dtype, tm=1024, tn=512):
    m, k = a.shape
    n = w_stack.shape[2]
    tm = min(tm, m)
    return pl.pallas_call(
        _mm_kernel, grid=(m // tm, n // tn),
        in_specs=[pl.BlockSpec((tm, k), lambda i, j: (i, 0)),
                  pl.BlockSpec((1, k, tn), lambda i, j: (layer, 0, j))],
        out_specs=pl.BlockSpec((tm, tn), lambda i, j: (i, j)),
        out_shape=jax.ShapeDtypeStruct((m, n), out_dtype),
        compiler_params=_cparams(2), name="in_proj",
    )(a, w_stack)


def _mm3_kernel(a1_ref, a2_ref, a3_ref, w_ref, o_ref):
    k1, k2 = a1_ref.shape[1], a2_ref.shape[1]
    acc = _dot(a1_ref[...], w_ref[0, 0:k1, :])
    acc = acc + _dot(a2_ref[...], w_ref[0, k1:k1 + k2, :])
    acc = acc + _dot(a3_ref[...], w_ref[0, k1 + k2:, :])
    o_ref[...] = acc.astype(o_ref.dtype)


def _out_proj(y_ret, y_m, y_f, w_stack, layer, out_dtype, tm=1024, tn=512):
    m = y_ret.shape[0]
    _, k, n = w_stack.shape
    tm = min(tm, m)
    return pl.pallas_call(
        _mm3_kernel, grid=(m // tm, n // tn),
        in_specs=[pl.BlockSpec((tm, y_ret.shape[1]), lambda i, j: (i, 0)),
                  pl.BlockSpec((tm, y_m.shape[1]), lambda i, j: (i, 0)),
                  pl.BlockSpec((tm, y_f.shape[1]), lambda i, j: (i, 0)),
                  pl.BlockSpec((1, k, tn), lambda i, j: (layer, 0, j))],
        out_specs=pl.BlockSpec((tm, tn), lambda i, j: (i, j)),
        out_shape=jax.ShapeDtypeStruct((m, n), out_dtype),
        compiler_params=_cparams(2), name="out_proj",
    )(y_ret, y_m, y_f, w_stack)


def _ret_kernel(q_ref, k_ref, v_ref, g_ref, cos_ref, sin_ref, decay_ref, zeta_ref, xi_ref, gam_ref,
                gn_ref, o_ref, state_ref, *, ts):
    @pl.when(pl.program_id(1) == 0)
    def _():
        state_ref[...] = jnp.zeros_like(state_ref)

    for c in range(ts // CHUNK):
        rows = slice(c * CHUNK, (c + 1) * CHUNK)
        cosv, sinv = cos_ref[rows, :], sin_ref[rows, :]
        for h in range(RET_HEADS):
            cols = slice(h * HEAD_DIM, (h + 1) * HEAD_DIM)
            q = q_ref[rows, cols].astype(F32)
            k = k_ref[rows, cols].astype(F32)
            qr = q * cosv + pltpu.roll(q, HEAD_DIM // 2, 1) * sinv
            kr = k * cosv + pltpu.roll(k, HEAD_DIM // 2, 1) * sinv
            qb, kb, vb = qr.astype(BF16), kr.astype(BF16), v_ref[rows, cols]
            s = _dot_nt(qb, kb) * decay_ref[h]
            state = state_ref[h]
            y = _dot(s.astype(BF16), vb) + _dot(qb, state.astype(BF16)) * xi_ref[h]
            kz_t = (kr * zeta_ref[h]).T.astype(BF16)
            state_ref[h] = state * gam_ref[h] + _dot(kz_t, vb)
            out = _group_norm(y, gn_ref[:, cols]) * _silu(g_ref[rows, cols].astype(F32))
            o_ref[rows, cols] = out.astype(o_ref.dtype)


def _retention(proj, gn_g, consts, *, batch, seq):
    ts = 512
    ns = seq // ts
    cosf, sins, decay, zeta, xi, gam = consts
    rowblk = lambda cb: pl.BlockSpec((ts, D_RET), lambda b, s, cb=cb: (b * ns + s, cb))
    full3 = pl.BlockSpec((RET_HEADS, CHUNK, HEAD_DIM), lambda b, s: (0, 0, 0))
    return pl.pallas_call(
        functools.partial(_ret_kernel, ts=ts), grid=(batch, ns),
        in_specs=[rowblk(COL_RQ // D_RET), rowblk(COL_RK // D_RET), rowblk(COL_RV // D_RET),
                  rowblk(COL_RG // D_RET),
                  pl.BlockSpec((ts, HEAD_DIM), lambda b, s: (s, 0)),
                  pl.BlockSpec((ts, HEAD_DIM), lambda b, s: (s, 0)),
                  full3, full3, full3,
                  pl.BlockSpec((RET_HEADS, 1, HEAD_DIM), lambda b, s: (0, 0, 0)),
                  pl.BlockSpec((1, D_RET), lambda b, s: (0, 0))],
        out_specs=pl.BlockSpec((ts, D_RET), lambda b, s: (b * ns + s, 0)),
        out_shape=jax.ShapeDtypeStruct((batch * seq, D_RET), BF16),
        scratch_shapes=[pltpu.VMEM((RET_HEADS, HEAD_DIM, HEAD_DIM), F32)],
        compiler_params=_cparams(2), name="retention",
    )(proj, proj, proj, proj, cosf, sins, decay, zeta, xi, gam, gn_g.reshape(1, D_RET))


def _retention_consts(seq):
    h = RET_HEADS
    log_gamma = jnp.log1p(-jnp.power(2.0, -5.0 - jnp.arange(h, dtype=F32)))
    pos = jnp.arange(CHUNK, dtype=F32)
    diff = pos[:, None] - pos[None, :]
    decay = jnp.where(diff >= 0, jnp.exp(jnp.maximum(diff, 0.0)[None] * log_gamma[:, None, None]), 0.0)
    zeta = jnp.exp((CHUNK - 1 - pos)[None, :] * log_gamma[:, None])
    xi = jnp.exp((pos + 1)[None, :] * log_gamma[:, None])
    gam = jnp.exp(CHUNK * log_gamma)
    bcast = lambda v: jnp.broadcast_to(v[:, :, None], (h, CHUNK, HEAD_DIM))
    half = HEAD_DIM // 2
    inv = ROPE_BASE ** (-jnp.arange(half, dtype=F32) / half)
    ang = jnp.arange(seq, dtype=F32)[:, None] * inv[None, :]
    cosf = jnp.concatenate([jnp.cos(ang), jnp.cos(ang)], axis=1)
    sins = jnp.concatenate([-jnp.sin(ang), jnp.sin(ang)], axis=1)
    gam_b = jnp.broadcast_to(gam[:, None, None], (h, 1, HEAD_DIM))
    return cosf, sins, decay, bcast(zeta), bcast(xi), gam_b


def _mlstm_kernel(x_ref, v_ref, og_ref, gates_ref, cw_ref, cb_ref, wq_ref, wk_ref, gn_ref, o_ref,
                  xbuf, c_state, n_state, m_state, *, ts):
    @pl.when(pl.program_id(1) == 0)
    def _():
        c_state[...] = jnp.zeros_like(c_state)
        n_state[...] = jnp.zeros_like(n_state)
        m_state[...] = jnp.full_like(m_state, NEG_INF)
        xbuf[0:8, :] = jnp.zeros((8, D_MLSTM), F32)

    x = x_ref[...].astype(F32)
    xbuf[8:ts + 8, :] = x
    conv = cb_ref[...]
    for j in range(CONV_WIDTH):
        start = 8 - (CONV_WIDTH - 1) + j
        conv = conv + cw_ref[j:j + 1, :] * xbuf[start:start + ts, :]
    xbuf[0:8, :] = x[ts - 8:ts, :]
    xc = _silu(conv).astype(BF16)

    tri = _lower_tri(CHUNK, BF16)
    r_i = lax.broadcasted_iota(jnp.int32, (CHUNK, CHUNK), 0)
    c_i = lax.broadcasted_iota(jnp.int32, (CHUNK, CHUNK), 1)
    causal = c_i <= r_i

    q_all = [_dot(xc[:, h * HEAD_DIM:(h + 1) * HEAD_DIM], wq_ref[h]).astype(BF16) for h in range(MLSTM_HEADS)]
    k_all = [_dot(xc[:, h * HEAD_DIM:(h + 1) * HEAD_DIM], wk_ref[h]) for h in range(MLSTM_HEADS)]

    lane8 = lax.broadcasted_iota(jnp.int32, (SUBLANES, CHUNK), 1)
    ones_cols = jnp.ones((CHUNK, HEAD_DIM), BF16)
    square = (CHUNK, CHUNK)

    def per_token_cols(row):
        return jnp.broadcast_to(row, square).T

    for c in range(ts // CHUNK):
        rows = slice(c * CHUNK, (c + 1) * CHUNK)
        gc = gates_ref[rows, :]
        b_col = _cumsum_rows(tri, _log_sigmoid(gc))
        li8 = gc.T[LANE_MI:LANE_MI + SUBLANES, :]
        b8 = pltpu.roll(b_col.T[0:SUBLANES, :], SUBLANES - LANE_MF, 0)
        g8 = li8 - b8
        cm8 = g8
        shift = 1
        while shift < CHUNK:
            cm8 = jnp.maximum(cm8, jnp.where(lane8 >= shift, pltpu.roll(cm8, shift, 1), NEG_INF))
            shift *= 2
        m_prev8 = m_state[...]
        big_m8 = jnp.maximum(cm8, m_prev8)
        m_t8 = b8 + big_m8
        b_end8 = jnp.broadcast_to(b8[:, CHUNK - 1:CHUNK], (SUBLANES, CHUNK))
        a8 = b_end8 + g8
        a_max8 = jnp.broadcast_to(jnp.max(a8, axis=1, keepdims=True), (SUBLANES, CHUNK))
        w8 = jnp.exp(a8 - a_max8)
        m_new8 = jnp.maximum(b_end8 + m_prev8, a_max8)
        s_old8 = jnp.exp(b_end8 + m_prev8 - m_new8)
        s_new8 = jnp.exp(a_max8 - m_new8)
        m_state[...] = m_new8

        for h in range(MLSTM_HEADS):
            cols = slice(h * HEAD_DIM, (h + 1) * HEAD_DIM)
            hrow = slice(h, h + 1)
            c_prev, n_prev = c_state[h], n_state[h]
            big_m = per_token_cols(big_m8[hrow, :])
            d_w = jnp.exp(jnp.where(causal, jnp.broadcast_to(g8[hrow, :], square) - big_m, NEG_INF))
            inter_w = jnp.exp(m_prev8[hrow, :] - big_m)
            floor = jnp.exp(-per_token_cols(m_t8[hrow, :]))

            q = q_all[h][rows, :]
            kf = k_all[h][rows, :]
            v_ones = jnp.concatenate([v_ref[rows, cols], ones_cols], axis=1)
            s = _dot_nt(q, kf.astype(BF16)) * d_w
            sv = _dot(s.astype(BF16), v_ones)
            qc = _dot(q, jnp.concatenate([c_prev, n_prev], axis=1).astype(BF16))
            num = sv[:, :HEAD_DIM] + inter_w * qc[:, :HEAD_DIM]
            den = sv[:, HEAD_DIM:] + inter_w * qc[:, HEAD_DIM:]
            hh = num / jnp.maximum(jnp.abs(den), floor)

            kw_t = (kf.T * w8[hrow, :]).astype(BF16)
            kv = _dot(kw_t, v_ones)
            c_state[h] = s_old8[hrow, :] * c_prev + s_new8[hrow, :] * kv[:, :HEAD_DIM]
            n_state[h] = s_old8[hrow, :] * n_prev + s_new8[hrow, :] * kv[:, HEAD_DIM:]

            out = jax.nn.sigmoid(og_ref[rows, cols].astype(F32)) * _group_norm(hh, gn_ref[:, cols])
            o_ref[rows, cols] = out.astype(o_ref.dtype)


def _mlstm(proj, gates, conv_w, conv_b, wq, wk, gn_g, *, batch, seq):
    ts = 512
    ns = seq // ts
    rowblk = lambda cb: pl.BlockSpec((ts, D_MLSTM), lambda b, s, cb=cb: (b * ns + s, cb))
    wspec = pl.BlockSpec((MLSTM_HEADS, HEAD_DIM, HEAD_DIM), lambda b, s: (0, 0, 0))
    return pl.pallas_call(
        functools.partial(_mlstm_kernel, ts=ts), grid=(batch, ns),
        in_specs=[rowblk(COL_MX // D_MLSTM), rowblk(COL_MV // D_MLSTM), rowblk(COL_MO // D_MLSTM),
                  pl.BlockSpec((ts, LANES), lambda b, s: (b * ns + s, 0)),
                  pl.BlockSpec((CONV_WIDTH, D_MLSTM), lambda b, s: (0, 0)),
                  pl.BlockSpec((1, D_MLSTM), lambda b, s: (0, 0)),
                  wspec, wspec,
                  pl.BlockSpec((1, D_MLSTM), lambda b, s: (0, 0))],
        out_specs=pl.BlockSpec((ts, D_MLSTM), lambda b, s: (b * ns + s, 0)),
        out_shape=jax.ShapeDtypeStruct((batch * seq, D_MLSTM), BF16),
        scratch_shapes=[pltpu.VMEM((ts + 8, D_MLSTM), F32),
                        pltpu.VMEM((MLSTM_HEADS, HEAD_DIM, HEAD_DIM), F32),
                        pltpu.VMEM((MLSTM_HEADS, HEAD_DIM, HEAD_DIM), F32),
                        pltpu.VMEM((SUBLANES, CHUNK), F32)],
        compiler_params=_cparams(2), name="mlstm",
    )(proj, proj, proj, gates, conv_w, conv_b.reshape(1, D_MLSTM), wq.astype(BF16),
      (wk * HEAD_DIM ** -0.5).astype(BF16), gn_g.reshape(1, D_MLSTM))


def _fcum_kernel(gates_ref, fcol_ref, frow_ref, carry_ref, *, ts):
    @pl.when(pl.program_id(1) == 0)
    def _():
        carry_ref[...] = jnp.zeros_like(carry_ref)

    tri = _lower_tri(CHUNK, BF16)
    carry = carry_ref[...]
    for c in range(ts // CHUNK):
        rows = slice(c * CHUNK, (c + 1) * CHUNK)
        cum = _cumsum_rows(tri, _log_sigmoid(gates_ref[rows, :])) + carry
        carry = cum[CHUNK - 1:CHUNK, :]
        fcol_ref[rows, :] = cum
        frow_ref[0, :, rows] = cum.T[LANE_FF:LANE_FF + FOX_HEADS, :]
    carry_ref[...] = carry


def _forget_cumsum(gates, *, batch, seq):
    ts = 512
    ns = seq // ts
    return pl.pallas_call(
        functools.partial(_fcum_kernel, ts=ts), grid=(batch, ns),
        in_specs=[pl.BlockSpec((ts, LANES), lambda b, s: (b * ns + s, 0))],
        out_specs=[pl.BlockSpec((ts, LANES), lambda b, s: (b * ns + s, 0)),
                   pl.BlockSpec((1, FOX_HEADS, ts), lambda b, s: (b, 0, s))],
        out_shape=[jax.ShapeDtypeStruct((batch * seq, LANES), F32),
                   jax.ShapeDtypeStruct((batch, FOX_HEADS, seq), F32)],
        scratch_shapes=[pltpu.VMEM((1, LANES), F32)],
        compiler_params=_cparams(2), name="forget_cumsum",
    )(gates)


def _fox_kernel(q_ref, k_ref, v_ref, fcol_ref, frow_ref, o_ref, s_ref, acc_ref, m_ref, l_ref, *, tq):
    h = pl.program_id(1)
    i = pl.program_id(2)
    q = q_ref[...]
    tk = tq
    lane = lax.broadcasted_iota(jnp.int32, (tq, LANES), 1)
    f_q = jnp.sum(jnp.where(lane == LANE_FF + h, fcol_ref[...], 0.0), axis=1, keepdims=True)
    f_q = jnp.broadcast_to(f_q, (tq, LANES))
    lane_tiles = lambda a: [a[:, c * LANES:(c + 1) * LANES] for c in range(a.shape[1] // LANES)]

    def scores(start, width):
        return _dot_nt(q, k_ref[pl.ds(start, width), :]) - frow_ref[0, :, pl.ds(start, width)]

    def keep(start, s):
        s_ref[:, pl.ds(start, s.shape[1])] = s
        m = m_ref[...]
        for s_c in lane_tiles(s):
            m = jnp.maximum(m, s_c)
        m_ref[...] = m

    def weigh(start, width):
        ps = [jnp.exp(s_c + shift) for s_c in lane_tiles(s_ref[:, pl.ds(start, width)])]
        l_ref[...] = l_ref[...] + sum(ps[1:], ps[0])
        p = jnp.concatenate(ps, axis=1).astype(BF16)
        acc_ref[...] = acc_ref[...] + _dot(p, v_ref[pl.ds(start, width), :])

    def sweep(n_keys, fn):
        n_wide = n_keys // tk

        def body(j, carry):
            fn(pl.multiple_of(j * tk, tk), tk)
            return carry

        lax.fori_loop(0, n_wide, body, 0)

        @pl.when(n_wide * tk < n_keys)
        def _():
            fn(pl.multiple_of(n_wide * tk, tq), tq)

    m_ref[...] = jnp.full_like(m_ref, NEG_INF)
    diag = pl.multiple_of(i * tq, tq)
    sweep(diag, lambda start, width: keep(start, scores(start, width)))
    r_i = lax.broadcasted_iota(jnp.int32, (tq, tq), 0)
    c_i = lax.broadcasted_iota(jnp.int32, (tq, tq), 1)
    keep(diag, jnp.where(c_i <= r_i, scores(diag, tq), NEG_INF))

    m_row = jnp.max(m_ref[...], axis=1, keepdims=True) + f_q
    shift = f_q - m_row
    acc_ref[...] = jnp.zeros_like(acc_ref)
    l_ref[...] = jnp.zeros_like(l_ref)
    sweep(diag + tq, weigh)
    l_row = jnp.sum(l_ref[...], axis=1, keepdims=True)
    o_ref[...] = (acc_ref[...] / l_row).astype(o_ref.dtype)


def _fox_attention(proj, fcol, frow, *, batch, seq):
    tq = min(1024, seq)
    nq = seq // tq
    cq, ck, cv = COL_FQ // HEAD_DIM, COL_FK // HEAD_DIM, COL_FV // HEAD_DIM
    return pl.pallas_call(
        functools.partial(_fox_kernel, tq=tq), grid=(batch, FOX_HEADS, nq),
        in_specs=[pl.BlockSpec((tq, HEAD_DIM), lambda b, h, i: (b * nq + i, cq + h)),
                  pl.BlockSpec((seq, HEAD_DIM), lambda b, h, i: (b, ck + h)),
                  pl.BlockSpec((seq, HEAD_DIM), lambda b, h, i: (b, cv + h)),
                  pl.BlockSpec((tq, LANES), lambda b, h, i: (b * nq + i, 0)),
                  pl.BlockSpec((1, 1, seq), lambda b, h, i: (b * FOX_HEADS + h, 0, 0))],
        out_specs=pl.BlockSpec((tq, HEAD_DIM), lambda b, h, i: (b * nq + i, h)),
        out_shape=jax.ShapeDtypeStruct((batch * seq, D_FOX), BF16),
        scratch_shapes=[pltpu.VMEM((tq, seq), F32), pltpu.VMEM((tq, HEAD_DIM), F32),
                        pltpu.VMEM((tq, LANES), F32), pltpu.VMEM((tq, LANES), F32)],
        compiler_params=_cparams(3), name="fox_attention",
    )(proj, proj, proj, fcol, frow.reshape(batch * FOX_HEADS, 1, seq))


def _route_kernel(logit_ref, eid_ref, gate_ref, rank_ref, cnt_ref, upper_ref, base_ref, *, tm):
    @pl.when(pl.program_id(0) == 0)
    def _():
        r = lax.broadcasted_iota(jnp.int32, (tm, tm), 0)
        c = lax.broadcasted_iota(jnp.int32, (tm, tm), 1)
        upper_ref[...] = jnp.where(r < c, 1.0, 0.0).astype(BF16)
        base_ref[...] = jnp.zeros_like(base_ref)

    lt = logit_ref[...].T
    e = lt[LANE_RE:LANE_RE + N_EXPERTS, :]
    g = lt[LANE_RG:LANE_RG + N_GROUPS, :]
    g_max = jnp.max(g, axis=0, keepdims=True)
    g_val = 1.0 / jnp.sum(jnp.exp(g - g_max), axis=0, keepdims=True)
    g_row = lax.broadcasted_iota(jnp.int32, (N_GROUPS, tm), 0).astype(F32)
    g_idx = jnp.min(jnp.where(g == g_max, g_row, float(N_GROUPS)), axis=0, keepdims=True)

    e_row_i = lax.broadcasted_iota(jnp.int32, (N_EXPERTS, tm), 0)
    e_row = e_row_i.astype(F32)
    e_grp = jnp.right_shift(e_row_i, EXPERTS_PER_GROUP.bit_length() - 1).astype(F32)
    el = jnp.where(e_grp == g_idx, e, NEG_INF)
    max1 = jnp.max(el, axis=0, keepdims=True)
    idx1 = jnp.min(jnp.where(el == max1, e_row, float(N_EXPERTS)), axis=0, keepdims=True)
    el2 = jnp.where(e_row == idx1, NEG_INF, el)
    max2 = jnp.max(el2, axis=0, keepdims=True)
    idx2 = jnp.min(jnp.where(el2 == max2, e_row, float(N_EXPERTS)), axis=0, keepdims=True)
    e_sum = jnp.sum(jnp.exp(el - max1), axis=0, keepdims=True)
    p1 = 1.0 / e_sum
    p2 = jnp.exp(max2 - max1) / e_sum
    p_sum = p1 + p2
    gate_ref[0:1, :] = g_val * (p1 / p_sum)
    gate_ref[1:2, :] = g_val * (p2 / p_sum)
    eid_ref[0:1, :] = idx1.astype(jnp.int32)
    eid_ref[1:2, :] = idx2.astype(jnp.int32)

    oh1 = e_row == idx1
    oh2 = e_row == idx2
    onehot = jnp.where(oh1 | oh2, 1.0, 0.0)
    before = _dot(onehot.astype(BF16), upper_ref[...]) + base_ref[:, 0:1]
    rank_ref[0:1, :] = jnp.sum(jnp.where(oh1, before, 0.0), axis=0, keepdims=True).astype(jnp.int32)
    rank_ref[1:2, :] = jnp.sum(jnp.where(oh2, before, 0.0), axis=0, keepdims=True).astype(jnp.int32)
    base_ref[...] = base_ref[...] + jnp.sum(onehot, axis=1, keepdims=True)
    cnt_ref[...] = base_ref[...]


def _route(logits):
    t = logits.shape[0]
    tm = min(1024, t)
    tok = lambda i: (0, i)
    return pl.pallas_call(
        functools.partial(_route_kernel, tm=tm), grid=(t // tm,),
        in_specs=[pl.BlockSpec((tm, LANES), lambda i: (i, 0))],
        out_specs=[pl.BlockSpec((2, tm), tok), pl.BlockSpec((2, tm), tok), pl.BlockSpec((2, tm), tok),
                   pl.BlockSpec((N_EXPERTS, LANES), lambda i: (0, 0))],
        out_shape=[jax.ShapeDtypeStruct((2, t), jnp.int32), jax.ShapeDtypeStruct((2, t), F32),
                   jax.ShapeDtypeStruct((2, t), jnp.int32),
                   jax.ShapeDtypeStruct((N_EXPERTS, LANES), F32)],
        scratch_shapes=[pltpu.VMEM((tm, tm), BF16), pltpu.VMEM((N_EXPERTS, LANES), F32)],
        compiler_params=_cparams(1), name="route",
    )(logits)


def _moe_kernel(tile_expert, n_active, row_id, h_hbm, w1_ref, w3_ref, w2_ref, o_hbm,
                xbuf, xb, ybuf, gsem, ssem, *, tm, nch, n_tokens, n_tiles):
    del tile_expert
    i = pl.program_id(0)
    n_act = n_active[0]
    last_step = jnp.minimum(n_act, n_tiles - 1)

    def token_of(rid):
        if n_tokens & (n_tokens - 1) == 0:
            return jnp.bitwise_and(rid, n_tokens - 1)
        return rid - n_tokens * ((rid >= n_tokens).astype(jnp.int32) + (rid >= 2 * n_tokens).astype(jnp.int32))

    def hbm_token(ref, tok):
        return ref.at[pl.ds(pl.multiple_of(tok * nch, nch), nch)]

    def vmem_row(ref, r):
        if isinstance(r, int):
            return ref.at[r // SUBLANES, :, r % SUBLANES, :]
        return ref.at[r // SUBLANES, :, lax.rem(r, SUBLANES), :]

    def start_gather(tile, r):
        tok = token_of(row_id[(tile + 1) * tm + r])
        pltpu.make_async_copy(hbm_token(h_hbm, tok), vmem_row(xbuf, r), gsem).start()

    def start_scatter(tile, r):
        dst = row_id[(tile + 1) * tm + r]
        pltpu.make_async_copy(vmem_row(ybuf, r), hbm_token(o_hbm, dst), ssem).start(priority=1)

    def wait_rows(sem):
        for _ in range(tm):
            pltpu.make_async_copy(hbm_token(h_hbm, 0), vmem_row(xbuf, 0), sem).wait()

    def rows_loop(fn):
        def body(r, carry):
            fn(r)
            return carry
        lax.fori_loop(0, tm, body, 0, unroll=8)

    @pl.when(i <= last_step)
    def _():
        @pl.when(i == 0)
        def _():
            ybuf[...] = jnp.zeros_like(ybuf)
            rows_loop(lambda r: start_gather(0, r))

        wait_rows(gsem)
        xb[...] = jnp.concatenate([xbuf[:, c].reshape(tm, LANES) for c in range(nch)], axis=1).astype(BF16)
        nxt = jnp.minimum(i + 1, n_tiles - 1)
        for r in range(tm):
            start_gather(nxt, r)
            start_scatter(i - 1, r)

        @pl.when(n_act >= i)
        def _():
            x = xb[...]
            h1 = _dot(x, w1_ref[0, 0].astype(BF16))
            h3 = _dot(x, w3_ref[0, 0].astype(BF16))
            act = (_silu(h1) * h3).astype(BF16)
            y = _dot(act, w2_ref[0, 0].astype(BF16))
            wait_rows(ssem)
            for c in range(nch):
                ybuf[:, c] = y[:, c * LANES:(c + 1) * LANES].reshape(tm // SUBLANES, SUBLANES, LANES)

        @pl.when(i == last_step)
        def _():
            wait_rows(gsem)

            @pl.when(i < n_act)
            def _():
                rows_loop(lambda r: start_scatter(i, r))
                wait_rows(ssem)


def _moe_experts(h2_tm, w1, w3, w2, layer, tile_expert, n_active, row_id, *, n_tiles):
    _, _, d, de = w1.shape
    nch = d // LANES
    t = h2_tm.shape[0] // nch
    tm = MOE_TILE
    wmap = lambda i, te, na, rid: (layer, te[i], 0, 0)
    grid_spec = pltpu.PrefetchScalarGridSpec(
        num_scalar_prefetch=3, grid=(n_tiles,),
        in_specs=[pl.BlockSpec(memory_space=pl.ANY),
                  pl.BlockSpec((1, 1, d, de), wmap), pl.BlockSpec((1, 1, d, de), wmap),
                  pl.BlockSpec((1, 1, de, d), wmap)],
        out_specs=pl.BlockSpec(memory_space=pl.ANY),
        scratch_shapes=[pltpu.VMEM((tm // SUBLANES, nch, SUBLANES, LANES), F32), pltpu.VMEM((tm, d), BF16),
                        pltpu.VMEM((tm // SUBLANES, nch, SUBLANES, LANES), F32),
                        pltpu.SemaphoreType.DMA(()), pltpu.SemaphoreType.DMA(())])
    return pl.pallas_call(
        functools.partial(_moe_kernel, tm=tm, nch=nch, n_tokens=t, n_tiles=n_tiles), grid_spec=grid_spec,
        out_shape=jax.ShapeDtypeStruct(((2 * t + tm) * nch, LANES), F32),
        compiler_params=_cparams(1), name="moe_experts",
    )(tile_expert, n_active, row_id, h2_tm, w1, w3, w2)


SMEM_1D_TILE = 1024


def _rowid_kernel(dest_ref, spare_hbm, o_ref, sem, *, n_pairs, tm):
    fill = pltpu.make_async_copy(spare_hbm, o_ref, sem)
    fill.start()
    fill.wait()

    def body(p, carry):
        o_ref[tm + dest_ref[p]] = p
        return carry

    lax.fori_loop(0, n_pairs, body, 0, unroll=8)


def _sorted_row_ids(dest, *, n_tokens, n_tiles):
    tm = MOE_TILE
    n_rows = -(-(n_tiles + 1) * tm // SMEM_1D_TILE) * SMEM_1D_TILE
    spare = 2 * n_tokens + jnp.arange(n_rows, dtype=jnp.int32) % tm
    return pl.pallas_call(
        functools.partial(_rowid_kernel, n_pairs=dest.shape[0], tm=tm),
        in_specs=[pl.BlockSpec(memory_space=pltpu.SMEM), pl.BlockSpec(memory_space=pl.ANY)],
        out_specs=pl.BlockSpec(memory_space=pltpu.SMEM),
        out_shape=jax.ShapeDtypeStruct((n_rows,), jnp.int32),
        scratch_shapes=[pltpu.SemaphoreType.DMA(())],
        name="sorted_row_ids",
    )(dest, spare)


def _dispatch_plan(eid, rank, counts, *, n_tokens, n_tiles):
    tm = MOE_TILE
    cnt = counts[:, 0].astype(jnp.int32)
    tiles_e = (cnt + tm - 1) // tm
    tile_end = jnp.cumsum(tiles_e)
    tile_start = tile_end - tiles_e
    experts = jnp.arange(N_EXPERTS, dtype=jnp.int32)
    row_off = jnp.sum(jnp.where(eid[..., None] == experts, tile_start * tm, 0), axis=-1)
    dest = row_off + rank
    n_active = tile_end[-1]
    tile_ids = jnp.arange(n_tiles, dtype=jnp.int32)
    last_tile = jnp.minimum(tile_ids, n_active - 1)
    tile_expert = jnp.sum((tile_end[None, :] <= last_tile[:, None]).astype(jnp.int32), axis=-1)
    tile_expert = jnp.minimum(tile_expert, N_EXPERTS - 1)
    row_id = _sorted_row_ids(dest.reshape(-1), n_tokens=n_tokens, n_tiles=n_tiles)
    return tile_expert, n_active.reshape(1), row_id


W_IN_GATES0 = COL_MO + D_MLSTM
W_IN_FOX0 = W_IN_GATES0 + 2 * MLSTM_HEADS
W_IN_GATES1 = W_IN_FOX0 + 3 * D_FOX


def _wprep_kernel(w_ref, o_ref, g_ref):
    s = HEAD_DIM ** -0.5
    w = w_ref[0]

    def put(dst, src):
        o_ref[0, :, dst:dst + src.shape[1]] = src.astype(BF16)

    put(COL_RQ, w[:, 0:COL_RK])
    put(COL_RK, w[:, COL_RK:COL_RV] * s)
    put(COL_RV, w[:, COL_RV:W_IN_GATES0])
    put(COL_FQ, w[:, W_IN_FOX0:W_IN_FOX0 + D_FOX] * s)
    put(COL_FK, w[:, W_IN_FOX0 + D_FOX:W_IN_GATES1])
    n_gate = 2 * MLSTM_HEADS + FOX_HEADS
    g_ref[0] = jnp.concatenate([w[:, W_IN_GATES0:W_IN_FOX0], w[:, W_IN_GATES1:W_IN_GATES1 + FOX_HEADS],
                                jnp.zeros((w.shape[0], LANES - n_gate), F32)], axis=1)


def _rearranged_w_in(w_in):
    depth, d, n_in = w_in.shape
    tr = 256
    return pl.pallas_call(
        _wprep_kernel, grid=(depth, d // tr),
        in_specs=[pl.BlockSpec((1, tr, n_in), lambda l, r: (l, r, 0))],
        out_specs=[pl.BlockSpec((1, tr, N_PROJ), lambda l, r: (l, r, 0)),
                   pl.BlockSpec((1, tr, LANES), lambda l, r: (l, r, 0))],
        out_shape=[jax.ShapeDtypeStruct((depth, d, N_PROJ), BF16),
                   jax.ShapeDtypeStruct((depth, d, LANES), F32)],
        compiler_params=_cparams(2), name="w_in_prep",
    )(w_in)


def _lane_pad(v):
    return jnp.pad(v, ((0, 0), (0, LANES - v.shape[-1])))


def kernel(x, c, ada_w, ada_b, norm1_g, w_in, ret_gn_g, mlstm_conv_w, mlstm_conv_b, mlstm_wq, mlstm_wk,
           mlstm_i_b, mlstm_f_b, mlstm_gn_g, fox_f_b, w_out, norm2_g, router_group_w, router_group_b,
           router_expert_w, router_expert_b, moe_w1, moe_w3, moe_w2, final_g):
    batch, seq, d = x.shape
    depth = ada_w.shape[0]
    t = batch * seq
    n_tiles = (2 * t) // MOE_TILE + N_EXPERTS

    mod = _modulation(c, ada_w, ada_b)
    mod = mod.reshape(depth, batch, N_MOD, 1, d)
    w_big, w_gate = _rearranged_w_in(w_in)
    gate_bias = _lane_pad(jnp.concatenate([mlstm_i_b, mlstm_f_b, fox_f_b], axis=-1))
    w_route = jnp.concatenate([router_expert_w, router_group_w], axis=-1)
    w_route = jnp.pad(w_route, ((0, 0), (0, 0), (0, LANES - w_route.shape[-1])))
    route_bias = _lane_pad(jnp.concatenate([router_expert_b, router_group_b], axis=-1))
    w_out_b = w_out.astype(BF16)
    ret_consts = _retention_consts(seq)

    xf = x.reshape(t, d)
    moe_out, tok_gates, g2_prev = None, (), None
    for l in range(depth):
        sh1, sc1, g1, sh2, sc2, g2 = (mod[l, :, k] for k in range(N_MOD))
        if l == 0:
            h1, gates = _norm_call(xf, (), (), None, norm1_g[l], sc1, sh1, w_gate[l], gate_bias[l:l + 1],
                                   seq=seq, write_x=False, side="x1", h_dtype=BF16)
        else:
            xf, h1, gates = _norm_call(xf, (moe_out, moe_out), tok_gates, g2_prev, norm1_g[l], sc1, sh1,
                                       w_gate[l], gate_bias[l:l + 1], seq=seq, write_x=True, side="x1",
                                       h_dtype=BF16, add_row_offsets=(0, t), add_tm=True)
        proj = _matmul(h1, w_big, l, BF16)
        y_ret = _retention(proj, ret_gn_g[l], ret_consts, batch=batch, seq=seq)
        y_m = _mlstm(proj, gates, mlstm_conv_w[l], mlstm_conv_b[l], mlstm_wq[l], mlstm_wk[l],
                     mlstm_gn_g[l], batch=batch, seq=seq)
        fcol, frow = _forget_cumsum(gates, batch=batch, seq=seq)
        y_f = _fox_attention(proj, fcol, frow, batch=batch, seq=seq)
        mix = _out_proj(y_ret, y_m, y_f, w_out_b, l, BF16)
        xf, h2, logits = _norm_call(xf, (mix,), (), g1, norm2_g[l], sc2, sh2, w_route[l],
                                    route_bias[l:l + 1], seq=seq, write_x=True, side="x3", h_dtype=F32,
                                    h_tm=True)
        eid, gate, rank, counts = _route(logits)
        tile_expert, n_active, row_id = _dispatch_plan(eid, rank, counts, n_tokens=t, n_tiles=n_tiles)
        moe_out = _moe_experts(h2, moe_w1, moe_w3, moe_w2, l, tile_expert, n_active, row_id,
                               n_tiles=n_tiles)
        tok_gates = tuple(jnp.broadcast_to(gate[k][:, None], (t, LANES)) for k in range(2))
        g2_prev = g2
    zeros = jnp.zeros((batch, 1, d), F32)
    (out,) = _norm_call(xf, (moe_out, moe_out), tok_gates, g2_prev, final_g, zeros, zeros, None, None,
                        seq=seq, write_x=False, side=None, h_dtype=F32,
                        add_row_offsets=(0, t), add_tm=True)
    return out.reshape(batch, seq, d)
```
